```python
import jax
import jax.numpy as jnp
from jax import lax
import numpy as np

D_MODEL = 2048
BATCH = 1
SEQ = 8192
DEPTH = 1
DEC_BATCH = 32
DEC_SEQ = 1
PAST_LEN = 16384
PAGE_SIZE = 128

N_META = 16
HEAD_DIM = 128
H_SB = 8
H_SP = 8
W_SB = H_SB * HEAD_DIM
W_SP = H_SP * HEAD_DIM
H_IDX = 16
D_IDX = 64
INDEX_TOPK = 256
D_FF = 5632
ROPE_THETA = 500000.0
QBLK = 128
EPS = 1e-6

OFF_Q_SB = 0
OFF_K_SB = OFF_Q_SB + W_SB
OFF_V_SB = OFF_K_SB + W_SB
OFF_Q_SP = OFF_V_SB + W_SB
OFF_K_SP = OFF_Q_SP + W_SP
OFF_V_SP = OFF_K_SP + W_SP
OFF_Q_IDX = OFF_V_SP + W_SP
OFF_K_IDX = OFF_Q_IDX + H_IDX * D_IDX
OFF_W_IDX = OFF_K_IDX + D_IDX
OFF_G_SB = OFF_W_IDX + H_IDX
OFF_G_SP = OFF_G_SB + D_MODEL
IN_COLS = OFF_G_SP + D_MODEL

kernel_name = "stickbreak_dsa_macaron_hybrid_step"


def _rms(x, g):
    xf = x.astype(jnp.float32)
    y = xf * lax.rsqrt(jnp.mean(xf * xf, axis=-1, keepdims=True) + EPS)
    return y.astype(x.dtype) * g


def _swiglu(x, w_gate, w_up, w_down):
    return (jax.nn.silu(x @ w_gate) * (x @ w_up)) @ w_down


def _rope_partial(x, pos):
    rot = x.shape[-1] // 4
    half = rot // 2
    inv_freq = ROPE_THETA ** (-jnp.arange(half, dtype=jnp.float32) / half)
    ang = pos.astype(jnp.float32)[:, None] * inv_freq[None, :]
    cos = jnp.cos(ang)[None, :, None, :]
    sin = jnp.sin(ang)[None, :, None, :]
    x1 = x[..., :half].astype(jnp.float32)
    x2 = x[..., half:rot].astype(jnp.float32)
    rotated = jnp.concatenate([x1 * cos - x2 * sin, x1 * sin + x2 * cos], axis=-1).astype(x.dtype)
    return jnp.concatenate([rotated, x[..., rot:]], axis=-1)


def _project(u, w_in, pos):
    b, s, _ = u.shape
    p = u @ w_in

    def heads(lo, n):
        return p[..., lo:lo + n * HEAD_DIM].reshape(b, s, n, HEAD_DIM)

    q_sb = heads(OFF_Q_SB, H_SB)
    k_sb = heads(OFF_K_SB, H_SB)
    v_sb = heads(OFF_V_SB, H_SB)
    q_sp = _rope_partial(heads(OFF_Q_SP, H_SP), pos)
    k_sp = _rope_partial(heads(OFF_K_SP, H_SP), pos)
    v_sp = heads(OFF_V_SP, H_SP)
    q_idx = _rope_partial(p[..., OFF_Q_IDX:OFF_K_IDX].reshape(b, s, H_IDX, D_IDX), pos)
    k_idx = _rope_partial(p[..., OFF_K_IDX:OFF_W_IDX].reshape(b, s, 1, D_IDX), pos)[:, :, 0]
    w_idx = p[..., OFF_W_IDX:OFF_G_SB] * (H_IDX ** -0.5)
    g_sb = jax.nn.sigmoid(p[..., OFF_G_SB:OFF_G_SP])
    g_sp = jax.nn.sigmoid(p[..., OFF_G_SP:IN_COLS])
    return q_sb, k_sb, v_sb, q_sp, k_sp, v_sp, q_idx, k_idx, w_idx, g_sb, g_sp


def _sb_attend(q, k, v, q_pos, k_pos):
    z = jnp.einsum('bqhd,bshd->bhqs', q.astype(jnp.float32), k.astype(jnp.float32)) * (HEAD_DIM ** -0.5)
    past = k_pos[None, :] < q_pos[:, None]
    log_keep = jnp.where(past, jax.nn.log_sigmoid(-z), 0.0)
    after = lax.cumsum(log_keep, axis=3, reverse=True) - log_keep
    a = jnp.where(past, jnp.exp(jax.nn.log_sigmoid(z) + after), 0.0)
    o = jnp.einsum('bhqs,bshd->bqhd', a, v.astype(jnp.float32))
    return o.astype(q.dtype)


def _index_topk(q_idx, w_idx, k_idx, q_pos, k_pos, topk):
    s = jnp.einsum('bqhd,bsd->bqhs', q_idx.astype(jnp.float32), k_idx.astype(jnp.float32)) * (D_IDX ** -0.5)
    score = jnp.einsum('bqh,bqhs->bqs', w_idx.astype(jnp.float32), jax.nn.relu(s))
    score = jnp.where(k_pos[None, None, :] <= q_pos[None, :, None], score, -jnp.inf)
    vals, idx = lax.top_k(score, topk)
    return idx, jnp.isfinite(vals)


def _sparse_attend(q, k_sel, v_sel, valid):
    logits = jnp.einsum('bqhd,bqkhd->bqhk', q.astype(jnp.float32), k_sel.astype(jnp.float32)) * (HEAD_DIM ** -0.5)
    logits = jnp.where(valid[:, :, None, :], logits, -jnp.inf)
    w = jax.nn.softmax(logits, axis=-1)
    return jnp.einsum('bqhk,bqkhd->bqhd', w, v_sel.astype(jnp.float32)).astype(q.dtype)


def _take_rows(rows, idx):
    return jax.vmap(lambda r, i: r[i])(rows, idx)


def _merge(o_sb, o_sp, g_sb, g_sp, w_proj_sb, w_proj_sp, w_out):
    b, s = o_sb.shape[:2]
    y_sb = o_sb.reshape(b, s, W_SB) @ w_proj_sb
    y_sp = o_sp.reshape(b, s, W_SP) @ w_proj_sp
    return (g_sb * y_sb + g_sp * y_sp) @ w_out


def _prompt_layer(h, lp, keep_meta):
    b, L, _ = h.shape
    n_real = L - N_META
    pos = jnp.arange(L, dtype=jnp.int32)
    topk = min(INDEX_TOPK, L // 4)
    h = h + 0.5 * _swiglu(_rms(h, lp['g_ffn1']), lp['w_ffn1_gate'], lp['w_ffn1_up'], lp['w_ffn1_down'])
    u = _rms(h, lp['g_mix'])
    q_sb, k_sb, v_sb, q_sp, k_sp, v_sp, q_idx, k_idx, w_idx, g_sb, g_sp = _project(u, lp['w_in'], pos)

    def block(start, nq):
        def rows(t):
            return lax.dynamic_slice_in_dim(t, start, nq, axis=1)
        q_pos = start + jnp.arange(nq, dtype=jnp.int32)
        o_sb = _sb_attend(rows(q_sb), k_sb, v_sb, q_pos, pos)
        idx, valid = _index_topk(rows(q_idx), rows(w_idx), k_idx, q_pos, pos, topk)
        o_sp = _sparse_attend(rows(q_sp), _take_rows(k_sp, idx), _take_rows(v_sp, idx), valid)
        return o_sb, o_sp

    o_sb, o_sp = lax.map(lambda i: block(N_META + i * QBLK, QBLK), jnp.arange(n_real // QBLK, dtype=jnp.int32))

    def unblock(o):
        return jnp.swapaxes(o, 0, 1).reshape(b, n_real, o.shape[-2], HEAD_DIM)

    o_sb, o_sp = unblock(o_sb), unblock(o_sp)
    first = N_META
    if keep_meta:
        m_sb, m_sp = block(0, N_META)
        o_sb = jnp.concatenate([m_sb, o_sb], axis=1)
        o_sp = jnp.concatenate([m_sp, o_sp], axis=1)
        first = 0
    h = h[:, first:] + _merge(o_sb, o_sp, g_sb[:, first:], g_sp[:, first:], lp['w_proj_sb'], lp['w_proj_sp'], lp['w_out'])
    h = h + 0.5 * _swiglu(_rms(h, lp['g_ffn2']), lp['w_ffn2_gate'], lp['w_ffn2_up'], lp['w_ffn2_down'])
    return h, (k_sb, v_sb, k_sp, v_sp, k_idx)


def _sample_layer(h, lp, l, cache_k_sb, cache_v_sb, cache_k_sp, cache_v_sp, cache_k_idx, page_table):
    pos = PAST_LEN + jnp.arange(DEC_SEQ, dtype=jnp.int32)
    k_pos = jnp.arange(PAST_LEN + DEC_SEQ, dtype=jnp.int32)
    topk = min(INDEX_TOPK, (PAST_LEN + DEC_SEQ) // 4)
    h = h + 0.5 * _swiglu(_rms(h, lp['g_ffn1']), lp['w_ffn1_gate'], lp['w_ffn1_up'], lp['w_ffn1_down'])
    u = _rms(h, lp['g_mix'])
    q_sb, k_sb, v_sb, q_sp, k_sp, v_sp, q_idx, k_idx, w_idx, g_sb, g_sp = _project(u, lp['w_in'], pos)

    def sb_one(args):
        pt, q, k_new, v_new = args
        k = jnp.concatenate([cache_k_sb[pt, l].reshape(PAST_LEN, H_SB, HEAD_DIM), k_new], axis=0)
        v = jnp.concatenate([cache_v_sb[pt, l].reshape(PAST_LEN, H_SB, HEAD_DIM), v_new], axis=0)
        return _sb_attend(q[None], k[None], v[None], pos, k_pos)[0]

    o_sb = lax.map(sb_one, (page_table, q_sb, k_sb, v_sb))

    k_idx_past = cache_k_idx[page_table, l].reshape(DEC_BATCH, PAST_LEN, D_IDX)
    idx, valid = _index_topk(q_idx, w_idx, jnp.concatenate([k_idx_past, k_idx], axis=1), pos, k_pos, topk)
    in_past = idx < PAST_LEN
    past_idx = jnp.minimum(idx, PAST_LEN - 1)
    phys = _take_rows(page_table, past_idx // PAGE_SIZE)
    off = past_idx % PAGE_SIZE
    new_idx = jnp.clip(idx - PAST_LEN, 0, DEC_SEQ - 1)

    def gather(cache, new_rows):
        return jnp.where(in_past[..., None, None], cache[phys, l, off], _take_rows(new_rows, new_idx))

    o_sp = _sparse_attend(q_sp, gather(cache_k_sp, k_sp), gather(cache_v_sp, v_sp), valid)
    h = h + _merge(o_sb, o_sp, g_sb, g_sp, lp['w_proj_sb'], lp['w_proj_sp'], lp['w_out'])
    h = h + 0.5 * _swiglu(_rms(h, lp['g_ffn2']), lp['w_ffn2_gate'], lp['w_ffn2_up'], lp['w_ffn2_down'])
    return h, (k_sb, v_sb, k_sp, v_sp, k_idx)


def setup_inputs(seed: int = 0) -> dict:
    key = jax.random.key(seed)
    ks = jax.random.split(key, 32)
    f32 = jnp.float32
    n_pages = PAST_LEN // PAGE_SIZE
    n_used = DEC_BATCH * n_pages
    n_pool = (n_used * 5) // 4

    def nrm(k, shape, scale=1.0):
        return jax.random.normal(k, shape, f32) * scale

    def gain(k, shape):
        return 1.0 + 0.01 * jax.random.normal(k, shape, f32)

    page_table = jax.random.permutation(ks[7], n_pool)[:n_used].reshape(DEC_BATCH, n_pages).astype(jnp.int32)
    return {
        'x_prompt': nrm(ks[0], (BATCH, SEQ, D_MODEL)),
        'x_sample': nrm(ks[1], (DEC_BATCH, DEC_SEQ, D_MODEL)),
        'cache_k_sb': nrm(ks[2], (n_pool, DEPTH, PAGE_SIZE, H_SB, HEAD_DIM)),
        'cache_v_sb': nrm(ks[3], (n_pool, DEPTH, PAGE_SIZE, H_SB, HEAD_DIM)),
        'cache_k_sp': nrm(ks[4], (n_pool, DEPTH, PAGE_SIZE, H_SP, HEAD_DIM)),
        'cache_v_sp': nrm(ks[5], (n_pool, DEPTH, PAGE_SIZE, H_SP, HEAD_DIM)),
        'cache_k_idx': nrm(ks[6], (n_pool, DEPTH, PAGE_SIZE, D_IDX)),
        'page_table': page_table,
        'meta_tokens': nrm(ks[8], (N_META, D_MODEL)),
        'g_ffn1': gain(ks[9], (DEPTH, D_MODEL)),
        'w_ffn1_gate': nrm(ks[10], (DEPTH, D_MODEL, D_FF), D_MODEL ** -0.5),
        'w_ffn1_up': nrm(ks[11], (DEPTH, D_MODEL, D_FF), D_MODEL ** -0.5),
        'w_ffn1_down': nrm(ks[12], (DEPTH, D_FF, D_MODEL), D_FF ** -0.5),
        'g_mix': gain(ks[13], (DEPTH, D_MODEL)),
        'w_in': nrm(ks[14], (DEPTH, D_MODEL, IN_COLS), D_MODEL ** -0.5),
        'w_proj_sb': nrm(ks[15], (DEPTH, W_SB, D_MODEL), W_SB ** -0.5),
        'w_proj_sp': nrm(ks[16], (DEPTH, W_SP, D_MODEL), W_SP ** -0.5),
        'w_out': nrm(ks[17], (DEPTH, D_MODEL, D_MODEL), D_MODEL ** -0.5),
        'g_ffn2': gain(ks[18], (DEPTH, D_MODEL)),
        'w_ffn2_gate': nrm(ks[19], (DEPTH, D_MODEL, D_FF), D_MODEL ** -0.5),
        'w_ffn2_up': nrm(ks[20], (DEPTH, D_MODEL, D_FF), D_MODEL ** -0.5),
        'w_ffn2_down': nrm(ks[21], (DEPTH, D_FF, D_MODEL), D_FF ** -0.5),
        'g_final': gain(ks[22], (D_MODEL,)),
    }


def reference(x_prompt, x_sample, cache_k_sb, cache_v_sb, cache_k_sp, cache_v_sp, cache_k_idx, page_table,
              meta_tokens, g_ffn1, w_ffn1_gate, w_ffn1_up, w_ffn1_down, g_mix, w_in, w_proj_sb, w_proj_sp,
              w_out, g_ffn2, w_ffn2_gate, w_ffn2_up, w_ffn2_down, g_final):
    meta = jnp.broadcast_to(meta_tokens[None].astype(x_prompt.dtype), (x_prompt.shape[0], N_META, D_MODEL))
    hp = jnp.concatenate([meta, x_prompt], axis=1)
    hs = x_sample
    rows_p = []
    rows_s = []
    for l in range(DEPTH):
        lp = {
            'g_ffn1': g_ffn1[l], 'w_ffn1_gate': w_ffn1_gate[l], 'w_ffn1_up': w_ffn1_up[l], 'w_ffn1_down': w_ffn1_down[l],
            'g_mix': g_mix[l], 'w_in': w_in[l], 'w_proj_sb': w_proj_sb[l], 'w_proj_sp': w_proj_sp[l], 'w_out': w_out[l],
            'g_ffn2': g_ffn2[l], 'w_ffn2_gate': w_ffn2_gate[l], 'w_ffn2_up': w_ffn2_up[l], 'w_ffn2_down': w_ffn2_down[l],
        }
        hp, new_p = _prompt_layer(hp, lp, l < DEPTH - 1)
        hs, new_s = _sample_layer(hs, lp, l, cache_k_sb, cache_v_sb, cache_k_sp, cache_v_sp, cache_k_idx, page_table)
        rows_p.append(new_p)
        rows_s.append(new_s)
    y_prompt = _rms(hp, g_final)
    y_sample = _rms(hs, g_final)

    def stack(rows, i):
        return jnp.stack([r[i] for r in rows], axis=1)

    return (y_prompt, y_sample,
            stack(rows_p, 0), stack(rows_p, 1), stack(rows_p, 2), stack(rows_p, 3), stack(rows_p, 4),
            stack(rows_s, 0), stack(rows_s, 1), stack(rows_s, 2), stack(rows_s, 3), stack(rows_s, 4))
```

```python
import functools

import jax
import jax.numpy as jnp
from jax import lax
from jax.experimental import pallas as pl
from jax.experimental.pallas import tpu as pltpu

D_MODEL = 2048
N_META = 16
HEAD_DIM = 128
H_SB = 8
H_SP = 8
W_SB = H_SB * HEAD_DIM
W_SP = H_SP * HEAD_DIM
H_IDX = 16
D_IDX = 64
INDEX_TOPK = 256
D_FF = 5632
ROPE_THETA = 500000.0
EPS = 1e-6
PAGE_SIZE = 128

LANES = 128
VMEM_LIMIT_BYTES = 56 * 1024 * 1024

ROW_TILE = 512
FF_TILE = 512
PROJ_TILE = 512
QBLK = 128
SB_KEY_TILE = 256
IDX_KEY_TILE = 512
SB_PAGES_PER_STEP = 8
IDX_PAGES_PER_STEP = 16

P_G_SB = 0
P_G_SP = P_G_SB + D_MODEL
P_Q_SB = P_G_SP + D_MODEL
P_K_SB = P_Q_SB + W_SB
P_V_SB = P_K_SB + W_SB
P_V_SP = P_V_SB + W_SB
P_Q_SP = P_V_SP + W_SP
P_K_SP = P_Q_SP + W_SP
P_Q_IDX = P_K_SP + W_SP
P_SMALL = P_Q_IDX + H_IDX * D_IDX
P_COLS = P_SMALL + PROJ_TILE
W_IDX_LANE = D_IDX

_J_PLAIN = P_Q_SB // PROJ_TILE
_J_ROPE128 = P_Q_SP // PROJ_TILE
_J_ROPE64 = P_Q_IDX // PROJ_TILE
_J_SMALL = P_SMALL // PROJ_TILE

INT_MIN = -(2 ** 31)
F32 = jnp.float32
BF16 = jnp.bfloat16


def _cparams(sem):
    return pltpu.CompilerParams(dimension_semantics=sem, vmem_limit_bytes=VMEM_LIMIT_BYTES)


def _rms(x):
    return x * lax.rsqrt(jnp.mean(x * x, axis=-1, keepdims=True) + EPS)


def _nt_dot(a, b):
    return lax.dot_general(a, b, (((1,), (1,)), ((), ())), preferred_element_type=F32)


def _dot(a, b):
    return jnp.dot(a, b, preferred_element_type=F32)


def _ffn_kernel(final, n_ff, h_ref, g_ref, wg_ref, wu_ref, wd_ref, *rest):
    if final:
        gf_ref, o_ref, u_s, acc_s = rest
    else:
        o_ref, u_s, acc_s = rest
    j = pl.program_id(1)

    @pl.when(j == 0)
    def _():
        u_s[...] = (_rms(h_ref[...]) * g_ref[...]).astype(BF16)
        acc_s[...] = jnp.zeros_like(acc_s)

    u = u_s[...]
    a = _dot(u, wg_ref[...])
    b = _dot(u, wu_ref[...])
    act = (a * jax.nn.sigmoid(a)) * b
    acc_s[...] += _dot(act.astype(BF16), wd_ref[...])

    @pl.when(j == n_ff - 1)
    def _():
        r = h_ref[...] + 0.5 * acc_s[...]
        if final:
            r = _rms(r) * gf_ref[...]
        o_ref[...] = r


def _ffn(h, n_rows, tm, gain, wg, wu, wd, g_final=None):
    final = g_final is not None
    n_ff = D_FF // FF_TILE
    in_specs = [
        pl.BlockSpec((tm, D_MODEL), lambda i, j: (i, 0)),
        pl.BlockSpec((1, D_MODEL), lambda i, j: (0, 0)),
        pl.BlockSpec((D_MODEL, FF_TILE), lambda i, j: (0, j)),
        pl.BlockSpec((D_MODEL, FF_TILE), lambda i, j: (0, j)),
        pl.BlockSpec((FF_TILE, D_MODEL), lambda i, j: (j, 0)),
    ]
    args = [h, gain, wg, wu, wd]
    if final:
        in_specs.append(pl.BlockSpec((1, D_MODEL), lambda i, j: (0, 0)))
        args.append(g_final)
    return pl.pallas_call(
        functools.partial(_ffn_kernel, final, n_ff),
        grid=(n_rows // tm, n_ff),
        in_specs=in_specs,
        out_specs=pl.BlockSpec((tm, D_MODEL), lambda i, j: (i, 0)),
        out_shape=jax.ShapeDtypeStruct((n_rows, D_MODEL), F32),
        scratch_shapes=[pltpu.VMEM((tm, D_MODEL), BF16), pltpu.VMEM((tm, D_MODEL), F32)],
        compiler_params=_cparams(("parallel", "arbitrary")),
        name="ffn_final" if final else "ffn",
    )(*args)


def _rotary(x, c, s1, s2, shift):
    return x * c + pltpu.roll(x, LANES - shift, 1) * s1 + pltpu.roll(x, shift, 1) * s2


def _proj_kernel(h_ref, g_ref, w_ref, t_ref, o_ref, u_s):
    j = pl.program_id(1)

    @pl.when(j == 0)
    def _():
        u_s[...] = (_rms(h_ref[...]) * g_ref[...]).astype(BF16)

    def tables(k):
        return [t_ref[:, (3 * k + m) * LANES:(3 * k + m + 1) * LANES] for m in range(3)]

    def rotary_tiles(y, k, shift, n_tiles):
        c, s1, s2 = tables(k)
        for t in range(PROJ_TILE // LANES):
            x = y[:, t * LANES:(t + 1) * LANES]
            if t < n_tiles:
                x = _rotary(x, c, s1, s2, shift)
            o_ref[:, t * LANES:(t + 1) * LANES] = x

    is_gate = j < _J_PLAIN
    is_plain = (j >= _J_PLAIN) & (j < _J_ROPE128)
    is_r128 = (j >= _J_ROPE128) & (j < _J_ROPE64)
    is_r64 = (j >= _J_ROPE64) & (j < _J_SMALL)
    is_small = j == _J_SMALL

    @pl.when(is_plain)
    def _():
        o_ref[...] = _dot(u_s[...], w_ref[...])

    @pl.when(is_r128)
    def _():
        rotary_tiles(_dot(u_s[...], w_ref[...]), 0, HEAD_DIM // 8, PROJ_TILE // LANES)

    @pl.when(is_r64)
    def _():
        rotary_tiles(_dot(u_s[...], w_ref[...]), 1, D_IDX // 8, PROJ_TILE // LANES)

    @pl.when(is_small)
    def _():
        rotary_tiles(_dot(u_s[...], w_ref[...]), 2, D_IDX // 8, 1)

    @pl.when(is_gate)
    def _():
        o_ref[...] = jax.nn.sigmoid(_dot(u_s[...], w_ref[...]))


def _project(h, gain, w_proj, tables, n_rows):
    return pl.pallas_call(
        _proj_kernel,
        grid=(n_rows // ROW_TILE, P_COLS // PROJ_TILE),
        in_specs=[
            pl.BlockSpec((ROW_TILE, D_MODEL), lambda i, j: (i, 0)),
            pl.BlockSpec((1, D_MODEL), lambda i, j: (0, 0)),
            pl.BlockSpec((D_MODEL, PROJ_TILE), lambda i, j: (0, j)),
            pl.BlockSpec((ROW_TILE, 9 * LANES), lambda i, j: (i, 0)),
        ],
        out_specs=pl.BlockSpec((ROW_TILE, PROJ_TILE), lambda i, j: (i, j)),
        out_shape=jax.ShapeDtypeStruct((n_rows, P_COLS), F32),
        scratch_shapes=[pltpu.VMEM((ROW_TILE, D_MODEL), BF16)],
        compiler_params=_cparams(("parallel", "arbitrary")),
        name="project",
    )(h, gain, w_proj, tables)


def _rope_tables(pos):
    n = pos.shape[0]
    posf = pos.astype(F32)

    def cs(half):
        inv_freq = ROPE_THETA ** (-jnp.arange(half, dtype=F32) / half)
        ang = posf[:, None] * inv_freq[None, :]
        return jnp.cos(ang), jnp.sin(ang)

    def z(w):
        return jnp.zeros((n, w), F32)

    def o(w):
        return jnp.ones((n, w), F32)

    c16, s16 = cs(HEAD_DIM // 8)
    c8, s8 = cs(D_IDX // 8)
    t128 = [jnp.concatenate([c16, c16, o(96)], 1), jnp.concatenate([-s16, z(112)], 1),
            jnp.concatenate([z(16), s16, z(96)], 1)]
    t64 = [jnp.tile(jnp.concatenate([c8, c8, o(48)], 1), (1, 2)), jnp.tile(jnp.concatenate([-s8, z(56)], 1), (1, 2)),
           jnp.tile(jnp.concatenate([z(8), s8, z(48)], 1), (1, 2))]
    tsm = [jnp.concatenate([c8, c8, o(48), (H_IDX ** -0.5) * o(16), o(48)], 1), jnp.concatenate([-s8, z(120)], 1),
           jnp.concatenate([z(8), s8, z(112)], 1)]
    return jnp.concatenate(t128 + t64 + tsm, axis=1)


def _strict_lower_ones(n):
    r = lax.broadcasted_iota(jnp.int32, (n, n), 0)
    c = lax.broadcasted_iota(jnp.int32, (n, n), 1)
    return jnp.where(r > c, 1.0, 0.0).astype(BF16)


def _sb_tile(q, kt, vt, tri, rsum, acc, valid):
    z = _nt_dot(q, kt) * (HEAD_DIM ** -0.5)
    l1p = jnp.log1p(jnp.exp(-jnp.abs(z)))
    log_beta = jnp.minimum(z, 0.0) - l1p
    log_keep = jnp.minimum(-z, 0.0) - l1p
    if valid is not None:
        log_keep = jnp.where(valid, log_keep, 0.0)
    hi = log_keep.astype(BF16)
    lo = (log_keep - hi.astype(F32)).astype(BF16)
    after = _dot(hi, tri) + _dot(lo, tri) + rsum
    a = jnp.exp(log_beta + after)
    if valid is not None:
        a = jnp.where(valid, a, 0.0)
    acc = acc + _dot(a.astype(BF16), vt)
    rsum = rsum + jnp.sum(log_keep, axis=1, keepdims=True)
    return rsum, acc


def _sb_prompt_kernel(seq, q_ref, k_ref, v_ref, o_ref):
    i = pl.program_id(1)
    tk = SB_KEY_TILE
    q = q_ref[...].astype(BF16)
    tri = _strict_lower_ones(tk)
    q_row = i * QBLK + lax.broadcasted_iota(jnp.int32, (QBLK, tk), 0)
    lane = lax.broadcasted_iota(jnp.int32, (QBLK, tk), 1)

    def load(start):
        return k_ref[pl.ds(start, tk), :].astype(BF16), v_ref[pl.ds(start, tk), :].astype(BF16)

    rsum = jnp.zeros((QBLK, 1), F32)
    acc = jnp.zeros((QBLK, HEAD_DIM), F32)

    c_diag = (i * QBLK) // tk
    d_start = pl.multiple_of(c_diag * tk, tk)
    kt, vt = load(d_start)
    rsum, acc = _sb_tile(q, kt, vt, tri, rsum, acc, (d_start + lane) < q_row)

    def body(it, carry):
        start = pl.multiple_of((c_diag - 1 - it) * tk, tk)
        kt, vt = load(start)
        return _sb_tile(q, kt, vt, tri, carry[0], carry[1], None)

    rsum, acc = lax.fori_loop(0, c_diag, body, (rsum, acc))

    kt, vt = load(seq)
    rsum, acc = _sb_tile(q, kt, vt, tri, rsum, acc, lane < N_META)
    o_ref[...] = acc


def _sb_prompt(p, seq, n_rows):
    nq = seq // QBLK
    col = lambda base: (lambda h, i: (0, base // HEAD_DIM + h))
    return pl.pallas_call(
        functools.partial(_sb_prompt_kernel, seq),
        grid=(H_SB, nq),
        in_specs=[
            pl.BlockSpec((QBLK, HEAD_DIM), lambda h, i: (i, P_Q_SB // HEAD_DIM + h)),
            pl.BlockSpec((n_rows, HEAD_DIM), col(P_K_SB)),
            pl.BlockSpec((n_rows, HEAD_DIM), col(P_V_SB)),
        ],
        out_specs=pl.BlockSpec((QBLK, HEAD_DIM), lambda h, i: (i, h)),
        out_shape=jax.ShapeDtypeStruct((seq, W_SB), F32),
        compiler_params=_cparams(("parallel", "arbitrary")),
        name="sb_prompt",
    )(p, p, p)


def _sortable_key(x):
    b = lax.bitcast_convert_type(x, jnp.int32)
    return jnp.where(b < 0, b ^ jnp.int32(0x7FFFFFFF), b)


def _idx_scores(qidx, wcols, ksmall):
    lane = lax.broadcasted_iota(jnp.int32, ksmall.shape, 1)
    k_even = jnp.where(lane < D_IDX, ksmall, 0.0)
    k_odd = pltpu.roll(k_even, D_IDX, 1)
    k_even = k_even.astype(BF16)
    k_odd = k_odd.astype(BF16)
    acc = None
    for hp in range(H_IDX // 2):
        qp = qidx[:, hp * LANES:(hp + 1) * LANES].astype(BF16)
        for par, kk in ((0, k_even), (1, k_odd)):
            s = _nt_dot(qp, kk) * (D_IDX ** -0.5)
            term = wcols[2 * hp + par] * jnp.maximum(s, 0.0)
            acc = term if acc is None else acc + term
    return acc


def _topk_threshold(count_ge, shape, topk):
    def body(b, p):
        cand = p + lax.shift_left(jnp.int32(1), jnp.int32(31) - b)
        return jnp.where(count_ge(cand) >= topk, cand, p)

    return lax.fori_loop(0, 32, body, jnp.full(shape, INT_MIN, jnp.int32))


def _idx_prompt_kernel(seq, topk, qidx_ref, qsmall_ref, ksmall_ref, bias_ref, keys_s):
    i = pl.program_id(0)
    tk = IDX_KEY_TILE
    n_own = (i * QBLK) // tk + 1
    meta_start = seq
    qidx = qidx_ref[...]
    qsmall = qsmall_ref[...]
    wcols = [qsmall[:, W_IDX_LANE + h:W_IDX_LANE + h + 1] for h in range(H_IDX)]
    q_row = i * QBLK + lax.broadcasted_iota(jnp.int32, (QBLK, tk), 0)
    lane = lax.broadcasted_iota(jnp.int32, (QBLK, tk), 1)

    def fill(start, valid):
        sc = _idx_scores(qidx, wcols, ksmall_ref[pl.ds(start, tk), :])
        keys_s[:, pl.ds(start, tk)] = jnp.where(valid, _sortable_key(sc), INT_MIN)

    def fill_body(c, _):
        start = pl.multiple_of(c * tk, tk)
        fill(start, (start + lane) <= q_row)
        return 0

    lax.fori_loop(0, n_own, fill_body, 0)
    fill(meta_start, lane < N_META)

    def count_ge(cand):
        cb = jnp.broadcast_to(cand, (QBLK, LANES))

        def add_tile(start, cnt):
            kt = keys_s[:, pl.ds(start, tk)]
            for t in range(tk // LANES):
                cnt = cnt + jnp.where(kt[:, t * LANES:(t + 1) * LANES] >= cb, 1.0, 0.0)
            return cnt

        cnt = lax.fori_loop(0, n_own, lambda c, cnt: add_tile(pl.multiple_of(c * tk, tk), cnt),
                            jnp.zeros((QBLK, LANES), F32))
        cnt = add_tile(meta_start, cnt)
        return jnp.sum(cnt, axis=1, keepdims=True)

    thr = _topk_threshold(count_ge, (QBLK, 1), float(topk))

    bias_ref[...] = jnp.full(bias_ref.shape, -jnp.inf, BF16)

    def emit(start):
        kt = keys_s[:, pl.ds(start, tk)]
        sel = (kt >= thr) & (kt != INT_MIN)
        bias_ref[:, pl.ds(start, tk)] = jnp.where(sel, 0.0, -jnp.inf).astype(BF16)

    def emit_body(c, _):
        emit(pl.multiple_of(c * tk, tk))
        return 0

    lax.fori_loop(0, n_own, emit_body, 0)
    emit(meta_start)


def _idx_prompt(p, seq, n_rows):
    nq = seq // QBLK
    topk = min(INDEX_TOPK, (seq + N_META) // 4)
    return pl.pallas_call(
        functools.partial(_idx_prompt_kernel, seq, topk),
        grid=(nq,),
        in_specs=[
            pl.BlockSpec((QBLK, H_IDX * D_IDX), lambda i: (i, P_Q_IDX // (H_IDX * D_IDX))),
            pl.BlockSpec((QBLK, LANES), lambda i: (i, P_SMALL // LANES)),
            pl.BlockSpec((n_rows, LANES), lambda i: (0, P_SMALL // LANES)),
        ],
        out_specs=pl.BlockSpec((QBLK, n_rows), lambda i: (i, 0)),
        out_shape=jax.ShapeDtypeStruct((seq, n_rows), BF16),
        scratch_shapes=[pltpu.VMEM((QBLK, n_rows), jnp.int32)],
        compiler_params=_cparams(("parallel",)),
        name="idx_prompt",
    )(p, p, p)


def _sp_prompt_kernel(seq, q_ref, k_ref, v_ref, bias_ref, o_ref):
    i = pl.program_id(1)
    tk = IDX_KEY_TILE
    n_own = (i * QBLK) // tk + 1
    q = q_ref[...].astype(BF16)

    def tile(start, carry):
        m, l, acc = carry
        kt = k_ref[pl.ds(start, tk), :].astype(BF16)
        vt = v_ref[pl.ds(start, tk), :].astype(BF16)
        s = _nt_dot(q, kt) * (HEAD_DIM ** -0.5) + bias_ref[:, pl.ds(start, tk)].astype(F32)
        m_new = jnp.maximum(m, jnp.max(s, axis=1, keepdims=True))
        alpha = jnp.exp(m - m_new)
        e = jnp.exp(s - m_new)
        l = alpha * l + jnp.sum(e, axis=1, keepdims=True)
        acc = alpha * acc + _dot(e.astype(BF16), vt)
        return m_new, l, acc

    carry = (jnp.full((QBLK, 1), -1e30, F32), jnp.zeros((QBLK, 1), F32), jnp.zeros((QBLK, HEAD_DIM), F32))
    carry = tile(seq, carry)
    carry = lax.fori_loop(0, n_own, lambda c, cr: tile(pl.multiple_of(c * tk, tk), cr), carry)
    o_ref[...] = carry[2] / carry[1]


def _sp_prompt(p, bias, seq, n_rows):
    nq = seq // QBLK
    col = lambda base: (lambda h, i: (0, base // HEAD_DIM + h))
    return pl.pallas_call(
        functools.partial(_sp_prompt_kernel, seq),
        grid=(H_SP, nq),
        in_specs=[
            pl.BlockSpec((QBLK, HEAD_DIM), lambda h, i: (i, P_Q_SP // HEAD_DIM + h)),
            pl.BlockSpec((n_rows, HEAD_DIM), col(P_K_SP)),
            pl.BlockSpec((n_rows, HEAD_DIM), col(P_V_SP)),
            pl.BlockSpec((QBLK, n_rows), lambda h, i: (i, 0)),
        ],
        out_specs=pl.BlockSpec((QBLK, HEAD_DIM), lambda h, i: (i, h)),
        out_shape=jax.ShapeDtypeStruct((seq, W_SP), F32),
        compiler_params=_cparams(("parallel", "arbitrary")),
        name="sp_prompt",
    )(p, p, p, bias)


Q_ROWS = 16


def _block_diag(qrow, n_heads):
    w = n_heads * HEAD_DIM
    r = lax.broadcasted_iota(jnp.int32, (Q_ROWS, w), 0)
    c = lax.broadcasted_iota(jnp.int32, (Q_ROWS, w), 1)
    return jnp.where((c // HEAD_DIM) == r, jnp.broadcast_to(qrow, (Q_ROWS, w)), 0.0)


def _diag_rows(x, n_heads):
    w = n_heads * HEAD_DIM
    r = lax.broadcasted_iota(jnp.int32, (Q_ROWS, w), 0)
    c = lax.broadcasted_iota(jnp.int32, (Q_ROWS, w), 1)
    return jnp.sum(jnp.where((c // HEAD_DIM) == r, x, 0.0), axis=0, keepdims=True)


def _sb_sample_kernel(n_steps, pt_ref, q_ref, *rest):
    npg = SB_PAGES_PER_STEP
    k_refs = rest[:npg]
    v_refs = rest[npg:2 * npg]
    o_ref, rsum_s, acc_s = rest[2 * npg:]
    j = pl.program_id(1)

    @pl.when(j == 0)
    def _():
        rsum_s[...] = jnp.zeros_like(rsum_s)
        acc_s[...] = jnp.zeros_like(acc_s)

    q = _block_diag(q_ref[0], H_SB).astype(BF16)
    tri = _strict_lower_ones(PAGE_SIZE)
    rsum = rsum_s[...]
    acc = acc_s[...]
    for k in range(npg):
        kt = k_refs[k][0].astype(BF16)
        vt = v_refs[k][0].astype(BF16)
        rsum, acc = _sb_tile(q, kt, vt, tri, rsum, acc, None)
    rsum_s[...] = rsum
    acc_s[...] = acc

    @pl.when(j == n_steps - 1)
    def _():
        o_ref[0] = _diag_rows(acc, H_SB)


def _sb_sample(page_table, q, cache_k, cache_v):
    nb, n_pages = page_table.shape
    npg = SB_PAGES_PER_STEP
    n_steps = n_pages // npg

    def page_spec(k):
        return pl.BlockSpec((1, PAGE_SIZE, W_SB), lambda b, j, pt: (pt[b, n_pages - 1 - (j * npg + k)], 0, 0))

    grid_spec = pltpu.PrefetchScalarGridSpec(
        num_scalar_prefetch=1,
        grid=(nb, n_steps),
        in_specs=[pl.BlockSpec((1, 1, W_SB), lambda b, j, pt: (b, 0, 0))]
        + [page_spec(k) for k in range(npg)] + [page_spec(k) for k in range(npg)],
        out_specs=pl.BlockSpec((1, 1, W_SB), lambda b, j, pt: (b, 0, 0)),
        scratch_shapes=[pltpu.VMEM((Q_ROWS, 1), F32), pltpu.VMEM((Q_ROWS, W_SB), F32)],
    )
    return pl.pallas_call(
        functools.partial(_sb_sample_kernel, n_steps),
        grid_spec=grid_spec,
        out_shape=jax.ShapeDtypeStruct((nb, 1, W_SB), F32),
        compiler_params=_cparams(("parallel", "arbitrary")),
        name="sb_sample",
    )(page_table, q, *([cache_k] * npg), *([cache_v] * npg))


def _idx_sample_kernel(n_steps, n_pages, topk, pt_ref, q_ref, w_ref, knew_ref, *rest):
    npg = IDX_PAGES_PER_STEP
    k_refs = rest[:npg]
    idx_ref, info_ref, sc_s, rank_s, sel_s = rest[npg:]
    j = pl.program_id(1)
    q = q_ref[0].astype(BF16)
    w = w_ref[0]

    def score(kp):
        s = _nt_dot(q, kp.astype(BF16)) * (D_IDX ** -0.5)
        return jnp.sum(w * jnp.maximum(s, 0.0), axis=0, keepdims=True)

    for k in range(npg):
        sc_s[pl.ds(j * npg + k, 1), :] = score(k_refs[k][0])

    @pl.when(j == n_steps - 1)
    def _():
        keys = _sortable_key(sc_s[...])
        s_new = jnp.sum(q.astype(F32) * knew_ref[0].astype(BF16).astype(F32), axis=1, keepdims=True)
        sc_new = jnp.sum(w * jnp.maximum(s_new * (D_IDX ** -0.5), 0.0), axis=0, keepdims=True)
        key_new = _sortable_key(sc_new)

        def total(x):
            return jnp.sum(jnp.sum(x, axis=1, keepdims=True), axis=0, keepdims=True)

        def count_ge(cand):
            return total(jnp.where(keys >= cand, 1.0, 0.0)) + jnp.where(key_new >= cand, 1.0, 0.0)

        thr = _topk_threshold(count_ge, (1, 1), float(topk))
        gt = jnp.where(keys > thr, 1.0, 0.0)
        eq = jnp.where(keys == thr, 1.0, 0.0)
        n_gt = total(gt) + jnp.where(key_new > thr, 1.0, 0.0)
        need = float(topk) - n_gt

        r = lax.broadcasted_iota(jnp.int32, (n_pages, n_pages), 0)
        c = lax.broadcasted_iota(jnp.int32, (n_pages, n_pages), 1)
        rows_before = jnp.where(c < r, 1.0, 0.0).astype(BF16)
        rl = lax.broadcasted_iota(jnp.int32, (PAGE_SIZE, PAGE_SIZE), 0)
        cl = lax.broadcasted_iota(jnp.int32, (PAGE_SIZE, PAGE_SIZE), 1)
        lanes_before = jnp.where(rl < cl, 1.0, 0.0).astype(BF16)

        def flat_rank(x):
            row_cnt = jnp.broadcast_to(jnp.sum(x, axis=1, keepdims=True), x.shape)
            return _dot(rows_before, row_cnt.astype(BF16)) + _dot(x.astype(BF16), lanes_before)

        sel = jnp.maximum(gt, jnp.where(flat_rank(eq) < need, eq, 0.0))
        n_past = total(sel)
        sel_new = jnp.where((key_new > thr) | ((key_new == thr) & (total(eq) < need)), 1.0, 0.0)
        rank_s[...] = flat_rank(sel)
        sel_s[...] = sel

        slot = lax.broadcasted_iota(jnp.int32, (topk, PAGE_SIZE), 0).astype(F32)
        lane = lax.broadcasted_iota(jnp.int32, (topk, PAGE_SIZE), 1)

        def gather_page(pg, acc):
            rk = jnp.broadcast_to(rank_s[pl.ds(pg, 1), :], (topk, PAGE_SIZE))
            sl = jnp.broadcast_to(sel_s[pl.ds(pg, 1), :], (topk, PAGE_SIZE))
            hit = jnp.where(rk == slot, sl, 0.0) > 0.5
            return acc + jnp.where(hit, (pg * PAGE_SIZE + lane).astype(F32), 0.0)

        pos = lax.fori_loop(0, n_pages, gather_page, jnp.zeros((topk, PAGE_SIZE), F32))
        idx_ref[0] = jnp.broadcast_to(jnp.sum(pos, axis=1, keepdims=True), (topk, LANES)).astype(jnp.int32)
        row = lax.broadcasted_iota(jnp.int32, (8, LANES), 0)
        info_ref[0] = jnp.where(row == 0, n_past, sel_new).astype(jnp.int32)


def _idx_sample(page_table, q, w, k_new, cache_k_idx):
    nb, n_pages = page_table.shape
    npg = IDX_PAGES_PER_STEP
    n_steps = n_pages // npg
    topk = min(INDEX_TOPK, (n_pages * PAGE_SIZE + 1) // 4)

    def page_spec(k):
        return pl.BlockSpec((1, PAGE_SIZE, D_IDX), lambda b, j, pt: (pt[b, j * npg + k], 0, 0))

    grid_spec = pltpu.PrefetchScalarGridSpec(
        num_scalar_prefetch=1,
        grid=(nb, n_steps),
        in_specs=[
            pl.BlockSpec((1, H_IDX, D_IDX), lambda b, j, pt: (b, 0, 0)),
            pl.BlockSpec((1, H_IDX, 1), lambda b, j, pt: (b, 0, 0)),
            pl.BlockSpec((1, 1, D_IDX), lambda b, j, pt: (b, 0, 0)),
        ] + [page_spec(k) for k in range(npg)],
        out_specs=[
            pl.BlockSpec((1, topk, LANES), lambda b, j, pt: (b, 0, 0)),
            pl.BlockSpec((1, 8, LANES), lambda b, j, pt: (b, 0, 0)),
        ],
        scratch_shapes=[pltpu.VMEM((n_pages, PAGE_SIZE), F32), pltpu.VMEM((n_pages, PAGE_SIZE), F32),
                        pltpu.VMEM((n_pages, PAGE_SIZE), F32)],
    )
    return pl.pallas_call(
        functools.partial(_idx_sample_kernel, n_steps, n_pages, topk),
        grid_spec=grid_spec,
        out_shape=[jax.ShapeDtypeStruct((nb, topk, LANES), jnp.int32),
                   jax.ShapeDtypeStruct((nb, 8, LANES), jnp.int32)],
        compiler_params=_cparams(("parallel", "arbitrary")),
        name="idx_sample",
    )(page_table, q, w, k_new, *([cache_k_idx] * npg))


def _sp_sample_kernel(topk, rows_ref, npast_ref, selnew_ref, q_ref, knew_ref, vnew_ref, k_hbm, v_hbm,
                      o_ref, kbuf, vbuf, sem):
    b = pl.program_id(0)
    nb = pl.num_programs(0)

    def row_copy(src, dst, slot, seq_b, r):
        row = rows_ref[seq_b * topk + r]
        return pltpu.make_async_copy(src.at[pl.ds(row, 1), :], dst.at[slot, pl.ds(r, 1), :], sem.at[slot])

    def start_all(seq_b, slot):
        def body(r, _):
            row_copy(k_hbm, kbuf, slot, seq_b, r).start()
            row_copy(v_hbm, vbuf, slot, seq_b, r).start()
            return 0
        lax.fori_loop(0, topk, body, 0)

    def wait_all(seq_b, slot):
        def body(r, _):
            row_copy(k_hbm, kbuf, slot, seq_b, r).wait()
            row_copy(v_hbm, vbuf, slot, seq_b, r).wait()
            return 0
        lax.fori_loop(0, topk, body, 0)

    slot = b % 2

    @pl.when(b == 0)
    def _():
        start_all(0, 0)

    @pl.when(b + 1 < nb)
    def _():
        start_all(b + 1, 1 - slot)

    wait_all(b, slot)

    qbd = _block_diag(q_ref[0], H_SP)
    scale = HEAD_DIM ** -0.5
    s = _nt_dot(qbd.astype(BF16), kbuf[slot].astype(BF16)) * scale
    lane = lax.broadcasted_iota(jnp.int32, s.shape, 1)
    s = jnp.where(lane < npast_ref[b], s, -jnp.inf)
    s_new = jnp.sum(qbd * knew_ref[0], axis=1, keepdims=True) * scale
    s_new = jnp.where(selnew_ref[b] > 0, s_new, -jnp.inf)
    m = jnp.maximum(jnp.max(s, axis=1, keepdims=True), s_new)
    e = jnp.exp(s - m)
    e_new = jnp.exp(s_new - m)
    denom = jnp.sum(e, axis=1, keepdims=True) + e_new
    acc = _dot(e.astype(BF16), vbuf[slot].astype(BF16)) + e_new * vnew_ref[0]
    o_ref[0] = _diag_rows(acc / denom, H_SP)


def _sp_sample(rows, n_past, sel_new, q, k_new, v_new, cache_k, cache_v, topk):
    nb = q.shape[0]
    vec = pl.BlockSpec((1, 1, W_SP), lambda b, *_: (b, 0, 0))
    grid_spec = pltpu.PrefetchScalarGridSpec(
        num_scalar_prefetch=3,
        grid=(nb,),
        in_specs=[vec, vec, vec, pl.BlockSpec(memory_space=pl.ANY), pl.BlockSpec(memory_space=pl.ANY)],
        out_specs=vec,
        scratch_shapes=[pltpu.VMEM((2, topk, W_SP), F32), pltpu.VMEM((2, topk, W_SP), F32),
                        pltpu.SemaphoreType.DMA((2,))],
    )
    return pl.pallas_call(
        functools.partial(_sp_sample_kernel, topk),
        grid_spec=grid_spec,
        out_shape=jax.ShapeDtypeStruct((nb, 1, W_SP), F32),
        compiler_params=_cparams(("arbitrary",)),
        name="sp_sample",
    )(rows, n_past, sel_new, q, k_new, v_new, cache_k, cache_v)


def _merge_kernel(osb_ref, osp_ref, gsb_ref, gsp_ref, h_ref, wsb_ref, wsp_ref, wout_ref, o_ref, y_s):
    j = pl.program_id(1)

    @pl.when(j == 0)
    def _():
        y_sb = _dot(osb_ref[...].astype(BF16), wsb_ref[...])
        y_sp = _dot(osp_ref[...].astype(BF16), wsp_ref[...])
        y_s[...] = (gsb_ref[...] * y_sb + gsp_ref[...] * y_sp).astype(BF16)

    o_ref[...] = h_ref[...] + _dot(y_s[...], wout_ref[...])


def _merge(o_sb, o_sp, g_sb, g_sp, h, w_sb, w_sp, w_out, n_rows, tm, g_cols):
    tn = 512
    return pl.pallas_call(
        _merge_kernel,
        grid=(n_rows // tm, D_MODEL // tn),
        in_specs=[
            pl.BlockSpec((tm, W_SB), lambda i, j: (i, 0)),
            pl.BlockSpec((tm, W_SP), lambda i, j: (i, 0)),
            pl.BlockSpec((tm, D_MODEL), lambda i, j: (i, g_cols[0])),
            pl.BlockSpec((tm, D_MODEL), lambda i, j: (i, g_cols[1])),
            pl.BlockSpec((tm, tn), lambda i, j: (i, j)),
            pl.BlockSpec((W_SB, D_MODEL), lambda i, j: (0, 0)),
            pl.BlockSpec((W_SP, D_MODEL), lambda i, j: (0, 0)),
            pl.BlockSpec((D_MODEL, tn), lambda i, j: (0, j)),
        ],
        out_specs=pl.BlockSpec((tm, tn), lambda i, j: (i, j)),
        out_shape=jax.ShapeDtypeStruct((n_rows, D_MODEL), F32),
        scratch_shapes=[pltpu.VMEM((tm, D_MODEL), BF16)],
        compiler_params=_cparams(("parallel", "arbitrary")),
        name="merge",
    )(o_sb, o_sp, g_sb, g_sp, h, w_sb, w_sp, w_out)


def kernel(x_prompt, x_sample, cache_k_sb, cache_v_sb, cache_k_sp, cache_v_sp, cache_k_idx, page_table, meta_tokens, g_ffn1, w_ffn1_gate, w_ffn1_up, w_ffn1_down, g_mix, w_in, w_proj_sb, w_proj_sp, w_out, g_ffn2, w_ffn2_gate, w_ffn2_up, w_ffn2_down, g_final):
    batch, seq, _ = x_prompt.shape
    nb, dec_seq, _ = x_sample.shape
    n_pool = cache_k_sb.shape[0]
    n_pages = page_table.shape[1]
    past_len = n_pages * PAGE_SIZE
    assert batch == 1 and dec_seq == 1 and g_ffn1.shape[0] == 1
    assert seq % ROW_TILE == 0 and N_META + nb <= ROW_TILE
    n_rows = seq + ROW_TILE
    samp0 = seq + N_META

    h0 = jnp.concatenate([x_prompt[0], meta_tokens.astype(F32), x_sample[:, 0, :],
                          jnp.zeros((n_rows - samp0 - nb, D_MODEL), F32)], axis=0)
    pos = jnp.concatenate([N_META + jnp.arange(seq, dtype=jnp.int32), jnp.arange(N_META, dtype=jnp.int32),
                           jnp.full((nb,), past_len, jnp.int32), jnp.zeros((n_rows - samp0 - nb,), jnp.int32)])
    tables = _rope_tables(pos)
    wi = w_in[0]
    o_k_idx = 7168
    o_g_sb = o_k_idx + D_IDX + H_IDX
    w_proj = jnp.concatenate([
        wi[:, o_g_sb:],
        wi[:, 0:3 * W_SB],
        wi[:, 3 * W_SB + 2 * W_SP:3 * W_SB + 3 * W_SP],
        wi[:, 3 * W_SB:3 * W_SB + 2 * W_SP],
        wi[:, 3 * W_SB + 3 * W_SP:o_k_idx],
        wi[:, o_k_idx:o_g_sb], jnp.zeros((D_MODEL, PROJ_TILE - D_IDX - H_IDX), F32),
    ], axis=1).astype(BF16)
    bf = lambda w: w[0].astype(BF16)
    row = lambda g: g.reshape(1, D_MODEL)

    h1 = _ffn(h0, n_rows, ROW_TILE, row(g_ffn1[0]), bf(w_ffn1_gate), bf(w_ffn1_up), bf(w_ffn1_down))
    p = _project(h1, row(g_mix[0]), w_proj, tables, n_rows)

    o_sb_p = _sb_prompt(p, seq, n_rows)
    bias = _idx_prompt(p, seq, n_rows)
    o_sp_p = _sp_prompt(p, bias, seq, n_rows)

    ps = p[samp0:samp0 + nb]
    seg = lambda base, w: ps[:, base:base + w]
    o_sb_s = _sb_sample(page_table, seg(P_Q_SB, W_SB).reshape(nb, 1, W_SB),
                        cache_k_sb.reshape(n_pool, PAGE_SIZE, W_SB), cache_v_sb.reshape(n_pool, PAGE_SIZE, W_SB))
    idx, info = _idx_sample(page_table, seg(P_Q_IDX, H_IDX * D_IDX).reshape(nb, H_IDX, D_IDX),
                            seg(P_SMALL + W_IDX_LANE, H_IDX).reshape(nb, H_IDX, 1),
                            seg(P_SMALL, D_IDX).reshape(nb, 1, D_IDX),
                            cache_k_idx.reshape(n_pool, PAGE_SIZE, D_IDX))
    topk = idx.shape[1]
    idx = idx[:, :, 0]
    rows = jnp.take_along_axis(page_table, idx // PAGE_SIZE, axis=1) * PAGE_SIZE + idx % PAGE_SIZE
    o_sp_s = _sp_sample(rows.reshape(-1), info[:, 0, 0], info[:, 1, 0],
                        seg(P_Q_SP, W_SP).reshape(nb, 1, W_SP), seg(P_K_SP, W_SP).reshape(nb, 1, W_SP),
                        seg(P_V_SP, W_SP).reshape(nb, 1, W_SP),
                        cache_k_sp.reshape(n_pool * PAGE_SIZE, W_SP), cache_v_sp.reshape(n_pool * PAGE_SIZE, W_SP), topk)

    wsb, wsp, wo = bf(w_proj_sb), bf(w_proj_sp), bf(w_out)
    ffn2 = (row(g_ffn2[0]), bf(w_ffn2_gate), bf(w_ffn2_up), bf(w_ffn2_down), row(g_final))
    h2_p = _merge(o_sb_p, o_sp_p, p, p, h1, wsb, wsp, wo, seq, ROW_TILE, (P_G_SB // D_MODEL, P_G_SP // D_MODEL))
    y_p = _ffn(h2_p, seq, ROW_TILE, *ffn2)
    h2_s = _merge(o_sb_s.reshape(nb, W_SB), o_sp_s.reshape(nb, W_SP), seg(P_G_SB, D_MODEL), seg(P_G_SP, D_MODEL),
                  h1[samp0:samp0 + nb], wsb, wsp, wo, nb, nb, (0, 0))
    y_s = _ffn(h2_s, nb, nb, *ffn2)

    def prompt_rows(base, w):
        return jnp.concatenate([p[seq:seq + N_META, base:base + w], p[:seq, base:base + w]], axis=0)

    def kv_p(base, nh):
        return prompt_rows(base, nh * HEAD_DIM).reshape(1, 1, seq + N_META, nh, HEAD_DIM)

    def kv_s(base, nh):
        return seg(base, nh * HEAD_DIM).reshape(nb, 1, 1, nh, HEAD_DIM)

    return (y_p.reshape(1, seq, D_MODEL), y_s.reshape(nb, 1, D_MODEL),
            kv_p(P_K_SB, H_SB), kv_p(P_V_SB, H_SB), kv_p(P_K_SP, H_SP), kv_p(P_V_SP, H_SP),
            prompt_rows(P_SMALL, D_IDX).reshape(1, 1, seq + N_META, D_IDX),
            kv_s(P_K_SB, H_SB), kv_s(P_V_SB, H_SB), kv_s(P_K_SP, H_SP), kv_s(P_V_SP, H_SP),
            seg(P_SMALL, D_IDX).reshape(nb, 1, 1, D_IDX))
```

```python
import functools

import jax
import jax.numpy as jnp
from jax import lax
from jax.experimental import pallas as pl
from jax.experimental.pallas import tpu as pltpu

D_MODEL = 2048
N_META = 16
HEAD_DIM = 128
H_SB = 8
H_SP = 8
W_SB = H_SB * HEAD_DIM
W_SP = H_SP * HEAD_DIM
H_IDX = 16
D_IDX = 64
INDEX_TOPK = 256
D_FF = 5632
ROPE_THETA = 500000.0
EPS = 1e-6
PAGE_SIZE = 128

LANES = 128
VMEM_LIMIT_BYTES = 56 * 1024 * 1024

ROW_TILE = 512
FF_TILE = 512
PROJ_TILE = 512
QBLK = 256
KEY_TILE = 512
TRI = 256
HEAD_GROUP = 2
COUNT_ROWS = 128
SB_PAGES_PER_STEP = 8
IDX_PAGES_PER_STEP = 16

P_G_SB = 0
P_G_SP = P_G_SB + D_MODEL
P_Q_SB = P_G_SP + D_MODEL
P_K_SB = P_Q_SB + W_SB
P_V_SB = P_K_SB + W_SB
P_V_SP = P_V_SB + W_SB
P_Q_SP = P_V_SP + W_SP
P_K_SP = P_Q_SP + W_SP
P_Q_IDX = P_K_SP + W_SP
P_SMALL = P_Q_IDX + H_IDX * D_IDX
P_COLS = P_SMALL + PROJ_TILE
W_IDX_LANE = D_IDX

_J_PLAIN = P_Q_SB // PROJ_TILE
_J_ROPE128 = P_Q_SP // PROJ_TILE
_J_ROPE64 = P_Q_IDX // PROJ_TILE
_J_SMALL = P_SMALL // PROJ_TILE

INT_MIN = -(2 ** 31)
F32 = jnp.float32
BF16 = jnp.bfloat16


def _cparams(sem):
    return pltpu.CompilerParams(dimension_semantics=sem, vmem_limit_bytes=VMEM_LIMIT_BYTES)


def _rms(x):
    return x * lax.rsqrt(jnp.mean(x * x, axis=-1, keepdims=True) + EPS)


def _nt_dot(a, b):
    return lax.dot_general(a, b, (((1,), (1,)), ((), ())), preferred_element_type=F32)


def _dot(a, b):
    return jnp.dot(a, b, preferred_element_type=F32)


def _ffn_kernel(final, n_ff, h_ref, g_ref, wg_ref, wu_ref, wd_ref, *rest):
    if final:
        gf_ref, o_ref, u_s, acc_s = rest
    else:
        o_ref, u_s, acc_s = rest
    j = pl.program_id(1)

    @pl.when(j == 0)
    def _():
        u_s[...] = (_rms(h_ref[...]) * g_ref[...]).astype(BF16)
        acc_s[...] = jnp.zeros_like(acc_s)

    u = u_s[...]
    a = _dot(u, wg_ref[...])
    b = _dot(u, wu_ref[...])
    act = (a * jax.nn.sigmoid(a)) * b
    acc_s[...] += _dot(act.astype(BF16), wd_ref[...])

    @pl.when(j == n_ff - 1)
    def _():
        r = h_ref[...] + 0.5 * acc_s[...]
        if final:
            r = _rms(r) * gf_ref[...]
        o_ref[...] = r


def _ffn(h, n_rows, tm, gain, wg, wu, wd, g_final=None):
    final = g_final is not None
    n_ff = D_FF // FF_TILE
    in_specs = [
        pl.BlockSpec((tm, D_MODEL), lambda i, j: (i, 0)),
        pl.BlockSpec((1, D_MODEL), lambda i, j: (0, 0)),
        pl.BlockSpec((D_MODEL, FF_TILE), lambda i, j: (0, j)),
        pl.BlockSpec((D_MODEL, FF_TILE), lambda i, j: (0, j)),
        pl.BlockSpec((FF_TILE, D_MODEL), lambda i, j: (j, 0)),
    ]
    args = [h, gain, wg, wu, wd]
    if final:
        in_specs.append(pl.BlockSpec((1, D_MODEL), lambda i, j: (0, 0)))
        args.append(g_final)
    return pl.pallas_call(
        functools.partial(_ffn_kernel, final, n_ff),
        grid=(n_rows // tm, n_ff),
        in_specs=in_specs,
        out_specs=pl.BlockSpec((tm, D_MODEL), lambda i, j: (i, 0)),
        out_shape=jax.ShapeDtypeStruct((n_rows, D_MODEL), F32),
        scratch_shapes=[pltpu.VMEM((tm, D_MODEL), BF16), pltpu.VMEM((tm, D_MODEL), F32)],
        compiler_params=_cparams(("parallel", "arbitrary")),
        name="ffn_final" if final else "ffn",
    )(*args)


def _rotary(x, c, s1, s2, shift):
    return x * c + pltpu.roll(x, LANES - shift, 1) * s1 + pltpu.roll(x, shift, 1) * s2


def _proj_kernel(h_ref, g_ref, w_ref, t_ref, o_ref, u_s):
    j = pl.program_id(1)

    @pl.when(j == 0)
    def _():
        u_s[...] = (_rms(h_ref[...]) * g_ref[...]).astype(BF16)

    def tables(k):
        return [t_ref[:, (3 * k + m) * LANES:(3 * k + m + 1) * LANES] for m in range(3)]

    def rotary_tiles(y, k, shift, n_tiles):
        c, s1, s2 = tables(k)
        for t in range(PROJ_TILE // LANES):
            x = y[:, t * LANES:(t + 1) * LANES]
            if t < n_tiles:
                x = _rotary(x, c, s1, s2, shift)
            o_ref[:, t * LANES:(t + 1) * LANES] = x

    is_gate = j < _J_PLAIN
    is_plain = (j >= _J_PLAIN) & (j < _J_ROPE128)
    is_r128 = (j >= _J_ROPE128) & (j < _J_ROPE64)
    is_r64 = (j >= _J_ROPE64) & (j < _J_SMALL)
    is_small = j == _J_SMALL

    @pl.when(is_plain)
    def _():
        o_ref[...] = _dot(u_s[...], w_ref[...])

    @pl.when(is_r128)
    def _():
        rotary_tiles(_dot(u_s[...], w_ref[...]), 0, HEAD_DIM // 8, PROJ_TILE // LANES)

    @pl.when(is_r64)
    def _():
        rotary_tiles(_dot(u_s[...], w_ref[...]), 1, D_IDX // 8, PROJ_TILE // LANES)

    @pl.when(is_small)
    def _():
        rotary_tiles(_dot(u_s[...], w_ref[...]), 2, D_IDX // 8, 1)

    @pl.when(is_gate)
    def _():
        o_ref[...] = jax.nn.sigmoid(_dot(u_s[...], w_ref[...]))


def _project(h, gain, w_proj, tables, n_rows):
    return pl.pallas_call(
        _proj_kernel,
        grid=(n_rows // ROW_TILE, P_COLS // PROJ_TILE),
        in_specs=[
            pl.BlockSpec((ROW_TILE, D_MODEL), lambda i, j: (i, 0)),
            pl.BlockSpec((1, D_MODEL), lambda i, j: (0, 0)),
            pl.BlockSpec((D_MODEL, PROJ_TILE), lambda i, j: (0, j)),
            pl.BlockSpec((ROW_TILE, 9 * LANES), lambda i, j: (i, 0)),
        ],
        out_specs=pl.BlockSpec((ROW_TILE, PROJ_TILE), lambda i, j: (i, j)),
        out_shape=jax.ShapeDtypeStruct((n_rows, P_COLS), F32),
        scratch_shapes=[pltpu.VMEM((ROW_TILE, D_MODEL), BF16)],
        compiler_params=_cparams(("parallel", "arbitrary")),
        name="project",
    )(h, gain, w_proj, tables)


def _rope_tables(pos):
    n = pos.shape[0]
    posf = pos.astype(F32)

    def cs(half):
        inv_freq = ROPE_THETA ** (-jnp.arange(half, dtype=F32) / half)
        ang = posf[:, None] * inv_freq[None, :]
        return jnp.cos(ang), jnp.sin(ang)

    def z(w):
        return jnp.zeros((n, w), F32)

    def o(w):
        return jnp.ones((n, w), F32)

    c16, s16 = cs(HEAD_DIM // 8)
    c8, s8 = cs(D_IDX // 8)
    t128 = [jnp.concatenate([c16, c16, o(96)], 1), jnp.concatenate([-s16, z(112)], 1),
            jnp.concatenate([z(16), s16, z(96)], 1)]
    t64 = [jnp.tile(jnp.concatenate([c8, c8, o(48)], 1), (1, 2)), jnp.tile(jnp.concatenate([-s8, z(56)], 1), (1, 2)),
           jnp.tile(jnp.concatenate([z(8), s8, z(48)], 1), (1, 2))]
    tsm = [jnp.concatenate([c8, c8, o(48), (H_IDX ** -0.5) * o(16), o(48)], 1), jnp.concatenate([-s8, z(120)], 1),
           jnp.concatenate([z(8), s8, z(112)], 1)]
    return jnp.concatenate(t128 + t64 + tsm, axis=1)


def _strict_lower_ones(n):
    r = lax.broadcasted_iota(jnp.int32, (n, n), 0)
    c = lax.broadcasted_iota(jnp.int32, (n, n), 1)
    return jnp.where(r > c, 1.0, 0.0).astype(BF16)


def _sb_logs(z):
    l1p = jnp.log1p(jnp.exp(-jnp.abs(z)))
    return jnp.minimum(z, 0.0) - l1p, jnp.minimum(-z, 0.0) - l1p


def _suffix_sums(x, tri):
    m, n = x.shape
    t = tri.shape[0]
    nsub = n // t
    hi = x.astype(BF16)
    lo = (x - hi.astype(F32)).astype(BF16)
    parts = []
    for c in range(nsub):
        parts += [hi[:, c * t:(c + 1) * t], lo[:, c * t:(c + 1) * t]]
    res = _dot(jnp.concatenate(parts, axis=0), tri)
    outs = []
    tail = None
    for c in reversed(range(nsub)):
        s = res[2 * c * m:(2 * c + 1) * m] + res[(2 * c + 1) * m:(2 * c + 2) * m]
        total = jnp.sum(x[:, c * t:(c + 1) * t], axis=1, keepdims=True)
        if tail is not None:
            s = s + tail
            total = total + tail
        outs.append(s)
        tail = total
    return jnp.concatenate(outs[::-1], axis=1), tail


def _sb_tile(q, kt, vt, tri, rsum, acc, valid):
    log_beta, log_keep = _sb_logs(_nt_dot(q, kt) * (HEAD_DIM ** -0.5))
    if valid is not None:
        log_keep = jnp.where(valid, log_keep, 0.0)
    after, total = _suffix_sums(log_keep, tri)
    a = jnp.exp(log_beta + (after + rsum))
    if valid is not None:
        a = jnp.where(valid, a, 0.0)
    return rsum + total, acc + _dot(a.astype(BF16), vt)


def _sb_prompt_kernel(seq, q_ref, k_ref, v_ref, tri_ref, o_ref):
    i = pl.program_id(1)
    tk = KEY_TILE
    tri = tri_ref[...]
    q_row = i * QBLK + lax.broadcasted_iota(jnp.int32, (QBLK, tk), 0)
    lane = lax.broadcasted_iota(jnp.int32, (QBLK, tk), 1)
    heads = [slice(g * HEAD_DIM, (g + 1) * HEAD_DIM) for g in range(HEAD_GROUP)]
    qs = [q_ref[:, hs].astype(BF16) for hs in heads]

    def tile(start, carry, valid):
        out = []
        for g, hs in enumerate(heads):
            kt = k_ref[pl.ds(start, tk), hs].astype(BF16)
            vt = v_ref[pl.ds(start, tk), hs].astype(BF16)
            out.append(_sb_tile(qs[g], kt, vt, tri, carry[g][0], carry[g][1], valid))
        return tuple(out)

    carry = tuple((jnp.zeros((QBLK, 1), F32), jnp.zeros((QBLK, HEAD_DIM), F32)) for _ in heads)

    c_diag = (i * QBLK) // tk
    d_start = pl.multiple_of(c_diag * tk, tk)
    carry = tile(d_start, carry, (d_start + lane) < q_row)
    carry = lax.fori_loop(0, c_diag, lambda it, cr: tile(pl.multiple_of((c_diag - 1 - it) * tk, tk), cr, None), carry)
    carry = tile(seq, carry, lane < N_META)
    for g, hs in enumerate(heads):
        o_ref[:, hs] = carry[g][1]


def _sb_prompt(p, seq, n_rows):
    gw = HEAD_GROUP * HEAD_DIM
    tri = jnp.tril(jnp.ones((TRI, TRI), F32), -1).astype(BF16)
    return pl.pallas_call(
        functools.partial(_sb_prompt_kernel, seq),
        grid=(H_SB // HEAD_GROUP, seq // QBLK),
        in_specs=[
            pl.BlockSpec((QBLK, gw), lambda h, i: (i, P_Q_SB // gw + h)),
            pl.BlockSpec((n_rows, gw), lambda h, i: (0, P_K_SB // gw + h)),
            pl.BlockSpec((n_rows, gw), lambda h, i: (0, P_V_SB // gw + h)),
            pl.BlockSpec((TRI, TRI), lambda h, i: (0, 0)),
        ],
        out_specs=pl.BlockSpec((QBLK, gw), lambda h, i: (i, h)),
        out_shape=jax.ShapeDtypeStruct((seq, W_SB), F32),
        compiler_params=_cparams(("parallel", "arbitrary")),
        name="sb_prompt",
    )(p, p, p, tri)


def _sortable_key(x):
    b = lax.bitcast_convert_type(x, jnp.int32)
    return jnp.where(b < 0, b ^ jnp.int32(0x7FFFFFFF), b)


def _idx_scores(qidx, wcols, ksmall):
    m = qidx.shape[0]
    n_pairs = H_IDX // 2
    lane = lax.broadcasted_iota(jnp.int32, ksmall.shape, 1)
    k_even = jnp.where(lane < D_IDX, ksmall, 0.0)
    k_odd = pltpu.roll(k_even, D_IDX, 1)
    q_stack = jnp.concatenate([qidx[:, hp * LANES:(hp + 1) * LANES] for hp in range(n_pairs)], axis=0)
    s_par = [_nt_dot(q_stack, kk.astype(BF16)) for kk in (k_even, k_odd)]
    acc = None
    for hp in range(n_pairs):
        for par in range(2):
            term = wcols[2 * hp + par] * jnp.maximum(s_par[par][hp * m:(hp + 1) * m], 0.0)
            acc = term if acc is None else acc + term
    return acc


def _topk_threshold(count_ge, shape, topk):
    def body(b, p):
        cand = p + lax.shift_left(jnp.int32(1), jnp.int32(31) - b)
        return jnp.where(count_ge(cand) >= topk, cand, p)

    return lax.fori_loop(0, 32, body, jnp.full(shape, INT_MIN, jnp.int32))


def _idx_prompt_kernel(seq, topk, qidx_ref, qsmall_ref, ksmall_ref, bias_ref, keys_s):
    i = pl.program_id(0)
    tk = KEY_TILE
    n_own = (i * QBLK + QBLK - 1) // tk + 1
    meta_start = seq
    qidx = qidx_ref[...].astype(BF16)
    qsmall = qsmall_ref[...] * (D_IDX ** -0.5)
    wcols = [qsmall[:, W_IDX_LANE + h:W_IDX_LANE + h + 1] for h in range(H_IDX)]
    q_row = i * QBLK + lax.broadcasted_iota(jnp.int32, (QBLK, tk), 0)
    lane = lax.broadcasted_iota(jnp.int32, (QBLK, tk), 1)

    def fill(start, valid):
        sc = _idx_scores(qidx, wcols, ksmall_ref[pl.ds(start, tk), :])
        keys_s[:, pl.ds(start, tk)] = jnp.where(valid, _sortable_key(sc), INT_MIN)

    def fill_body(c, _):
        start = pl.multiple_of(c * tk, tk)
        fill(start, (start + lane) <= q_row)
        return 0

    lax.fori_loop(0, n_own, fill_body, 0)
    fill(meta_start, lane < N_META)

    def count_ge(cand):
        counts = []
        for r0 in range(0, QBLK, COUNT_ROWS):
            cb = jnp.broadcast_to(cand[r0:r0 + COUNT_ROWS], (COUNT_ROWS, LANES))

            def add_tile(start, cnt, r0=r0, cb=cb):
                kt = keys_s[pl.ds(r0, COUNT_ROWS), pl.ds(start, tk)]
                for t in range(tk // LANES):
                    cnt = cnt + jnp.where(kt[:, t * LANES:(t + 1) * LANES] >= cb, 1.0, 0.0)
                return cnt

            cnt = lax.fori_loop(0, n_own, lambda c, cnt: add_tile(pl.multiple_of(c * tk, tk), cnt),
                                jnp.zeros((COUNT_ROWS, LANES), F32))
            cnt = add_tile(meta_start, cnt)
            counts.append(jnp.sum(cnt, axis=1, keepdims=True))
        return jnp.concatenate(counts, axis=0)

    thr = _topk_threshold(count_ge, (QBLK, 1), float(topk))

    bias_ref[...] = jnp.full(bias_ref.shape, -jnp.inf, BF16)

    def emit(start):
        kt = keys_s[:, pl.ds(start, tk)]
        sel = (kt >= thr) & (kt != INT_MIN)
        bias_ref[:, pl.ds(start, tk)] = jnp.where(sel, 0.0, -jnp.inf).astype(BF16)

    def emit_body(c, _):
        emit(pl.multiple_of(c * tk, tk))
        return 0

    lax.fori_loop(0, n_own, emit_body, 0)
    emit(meta_start)


def _idx_prompt(p, seq, n_rows):
    nq = seq // QBLK
    topk = min(INDEX_TOPK, (seq + N_META) // 4)
    return pl.pallas_call(
        functools.partial(_idx_prompt_kernel, seq, topk),
        grid=(nq,),
        in_specs=[
            pl.BlockSpec((QBLK, H_IDX * D_IDX), lambda i: (i, P_Q_IDX // (H_IDX * D_IDX))),
            pl.BlockSpec((QBLK, LANES), lambda i: (i, P_SMALL // LANES)),
            pl.BlockSpec((n_rows, LANES), lambda i: (0, P_SMALL // LANES)),
        ],
        out_specs=pl.BlockSpec((QBLK, n_rows), lambda i: (i, 0)),
        out_shape=jax.ShapeDtypeStruct((seq, n_rows), BF16),
        scratch_shapes=[pltpu.VMEM((QBLK, n_rows), jnp.int32)],
        compiler_params=_cparams(("parallel",)),
        name="idx_prompt",
    )(p, p, p)


def _sp_prompt_kernel(seq, q_ref, k_ref, v_ref, bias_ref, o_ref):
    i = pl.program_id(1)
    tk = KEY_TILE
    n_own = (i * QBLK + QBLK - 1) // tk + 1
    heads = [slice(g * HEAD_DIM, (g + 1) * HEAD_DIM) for g in range(HEAD_GROUP)]
    qs = [q_ref[:, hs].astype(BF16) for hs in heads]

    def tile(start, carry):
        bias = bias_ref[:, pl.ds(start, tk)].astype(F32)
        out = []
        for g, hs in enumerate(heads):
            m, l, acc = carry[g]
            kt = k_ref[pl.ds(start, tk), hs].astype(BF16)
            vt = v_ref[pl.ds(start, tk), hs].astype(BF16)
            s = _nt_dot(qs[g], kt) * (HEAD_DIM ** -0.5) + bias
            m_new = jnp.maximum(m, jnp.max(s, axis=1, keepdims=True))
            alpha = jnp.exp(m - m_new)
            e = jnp.exp(s - m_new)
            l = alpha * l + jnp.sum(e, axis=1, keepdims=True)
            acc = alpha * acc + _dot(e.astype(BF16), vt)
            out.append((m_new, l, acc))
        return tuple(out)

    carry = tuple((jnp.full((QBLK, 1), -1e30, F32), jnp.zeros((QBLK, 1), F32), jnp.zeros((QBLK, HEAD_DIM), F32))
                  for _ in heads)
    carry = tile(seq, carry)
    carry = lax.fori_loop(0, n_own, lambda c, cr: tile(pl.multiple_of(c * tk, tk), cr), carry)
    for g, hs in enumerate(heads):
        o_ref[:, hs] = carry[g][2] / carry[g][1]


def _sp_prompt(p, bias, seq, n_rows):
    gw = HEAD_GROUP * HEAD_DIM
    return pl.pallas_call(
        functools.partial(_sp_prompt_kernel, seq),
        grid=(H_SP // HEAD_GROUP, seq // QBLK),
        in_specs=[
            pl.BlockSpec((QBLK, gw), lambda h, i: (i, P_Q_SP // gw + h)),
            pl.BlockSpec((n_rows, gw), lambda h, i: (0, P_K_SP // gw + h)),
            pl.BlockSpec((n_rows, gw), lambda h, i: (0, P_V_SP // gw + h)),
            pl.BlockSpec((QBLK, n_rows), lambda h, i: (i, 0)),
        ],
        out_specs=pl.BlockSpec((QBLK, gw), lambda h, i: (i, h)),
        out_shape=jax.ShapeDtypeStruct((seq, W_SP), F32),
        compiler_params=_cparams(("parallel", "arbitrary")),
        name="sp_prompt",
    )(p, p, p, bias)


Q_ROWS = 16


def _block_diag(qrow, n_heads):
    w = n_heads * HEAD_DIM
    r = lax.broadcasted_iota(jnp.int32, (Q_ROWS, w), 0)
    c = lax.broadcasted_iota(jnp.int32, (Q_ROWS, w), 1)
    return jnp.where((c // HEAD_DIM) == r, jnp.broadcast_to(qrow, (Q_ROWS, w)), 0.0)


def _diag_rows(x, n_heads):
    w = n_heads * HEAD_DIM
    r = lax.broadcasted_iota(jnp.int32, (Q_ROWS, w), 0)
    c = lax.broadcasted_iota(jnp.int32, (Q_ROWS, w), 1)
    return jnp.sum(jnp.where((c // HEAD_DIM) == r, x, 0.0), axis=0, keepdims=True)


def _page_rows(ref, n_heads, lead=(0, 0)):
    r = ref.at[lead]
    t = r.shape[0]
    flat = r.reshape(t * n_heads, HEAD_DIM)
    return jnp.concatenate([flat[pl.ds(h, t, stride=n_heads), :] for h in range(n_heads)], axis=1)


def _sb_sample_kernel(n_steps, pt_ref, q_ref, *rest):
    npg = SB_PAGES_PER_STEP
    k_refs = rest[:npg]
    v_refs = rest[npg:2 * npg]
    o_ref, rsum_s, acc_s = rest[2 * npg:]
    j = pl.program_id(1)

    @pl.when(j == 0)
    def _():
        rsum_s[...] = jnp.zeros_like(rsum_s)
        acc_s[...] = jnp.zeros_like(acc_s)

    q = _block_diag(q_ref[0], H_SB).astype(BF16)
    tri = _strict_lower_ones(PAGE_SIZE)
    z = jnp.concatenate([_nt_dot(q, _page_rows(k_refs[k], H_SB).astype(BF16)) for k in range(npg)], axis=0)
    log_beta, log_keep = _sb_logs(z * (HEAD_DIM ** -0.5))
    after, _ = _suffix_sums(log_keep, tri)
    totals = jnp.sum(log_keep, axis=1, keepdims=True)
    rsum = rsum_s[...]
    carried = []
    for k in range(npg):
        carried.append(rsum)
        rsum = rsum + totals[k * Q_ROWS:(k + 1) * Q_ROWS]
    rsum_s[...] = rsum
    a = jnp.exp(log_beta + (after + jnp.concatenate(carried, axis=0))).astype(BF16)
    acc = acc_s[...]
    for k in range(npg):
        acc = acc + _dot(a[k * Q_ROWS:(k + 1) * Q_ROWS], _page_rows(v_refs[k], H_SB).astype(BF16))
    acc_s[...] = acc

    @pl.when(j == n_steps - 1)
    def _():
        o_ref[0] = _diag_rows(acc, H_SB)


def _sb_sample(page_table, q, cache_k, cache_v):
    nb, n_pages = page_table.shape
    npg = SB_PAGES_PER_STEP
    n_steps = n_pages // npg

    def page_spec(k):
        return pl.BlockSpec((1, 1, PAGE_SIZE, H_SB, HEAD_DIM),
                            lambda b, j, pt: (pt[b, n_pages - 1 - (j * npg + k)], 0, 0, 0, 0))

    grid_spec = pltpu.PrefetchScalarGridSpec(
        num_scalar_prefetch=1,
        grid=(nb, n_steps),
        in_specs=[pl.BlockSpec((1, 1, W_SB), lambda b, j, pt: (b, 0, 0))]
        + [page_spec(k) for k in range(npg)] + [page_spec(k) for k in range(npg)],
        out_specs=pl.BlockSpec((1, 1, W_SB), lambda b, j, pt: (b, 0, 0)),
        scratch_shapes=[pltpu.VMEM((Q_ROWS, 1), F32), pltpu.VMEM((Q_ROWS, W_SB), F32)],
    )
    return pl.pallas_call(
        functools.partial(_sb_sample_kernel, n_steps),
        grid_spec=grid_spec,
        out_shape=jax.ShapeDtypeStruct((nb, 1, W_SB), F32),
        compiler_params=_cparams(("parallel", "arbitrary")),
        name="sb_sample",
    )(page_table, q, *([cache_k] * npg), *([cache_v] * npg))


def _idx_sample_kernel(n_steps, n_pages, topk, pt_ref, q_ref, w_ref, knew_ref, *rest):
    npg = IDX_PAGES_PER_STEP
    k_refs = rest[:npg]
    idx_ref, info_ref, sc_s, rank_s, sel_s = rest[npg:]
    j = pl.program_id(1)
    q = q_ref[0].astype(BF16)
    w = w_ref[0]

    def score(kp):
        s = _nt_dot(q, kp.astype(BF16)) * (D_IDX ** -0.5)
        return jnp.sum(w * jnp.maximum(s, 0.0), axis=0, keepdims=True)

    for k in range(npg):
        sc_s[pl.ds(j * npg + k, 1), :] = score(k_refs[k][0])

    @pl.when(j == n_steps - 1)
    def _():
        keys = _sortable_key(sc_s[...])
        s_new = jnp.sum(q.astype(F32) * knew_ref[0].astype(BF16).astype(F32), axis=1, keepdims=True)
        sc_new = jnp.sum(w * jnp.maximum(s_new * (D_IDX ** -0.5), 0.0), axis=0, keepdims=True)
        key_new = _sortable_key(sc_new)

        def total(x):
            return jnp.sum(jnp.sum(x, axis=1, keepdims=True), axis=0, keepdims=True)

        def count_ge(cand):
            return total(jnp.where(keys >= cand, 1.0, 0.0)) + jnp.where(key_new >= cand, 1.0, 0.0)

        thr = _topk_threshold(count_ge, (1, 1), float(topk))
        gt = jnp.where(keys > thr, 1.0, 0.0)
        eq = jnp.where(keys == thr, 1.0, 0.0)
        n_gt = total(gt) + jnp.where(key_new > thr, 1.0, 0.0)
        need = float(topk) - n_gt

        r = lax.broadcasted_iota(jnp.int32, (n_pages, n_pages), 0)
        c = lax.broadcasted_iota(jnp.int32, (n_pages, n_pages), 1)
        rows_before = jnp.where(c < r, 1.0, 0.0).astype(BF16)
        rl = lax.broadcasted_iota(jnp.int32, (PAGE_SIZE, PAGE_SIZE), 0)
        cl = lax.broadcasted_iota(jnp.int32, (PAGE_SIZE, PAGE_SIZE), 1)
        lanes_before = jnp.where(rl < cl, 1.0, 0.0).astype(BF16)

        def flat_rank(x):
            row_cnt = jnp.broadcast_to(jnp.sum(x, axis=1, keepdims=True), x.shape)
            return _dot(rows_before, row_cnt.astype(BF16)) + _dot(x.astype(BF16), lanes_before)

        sel = jnp.maximum(gt, jnp.where(flat_rank(eq) < need, eq, 0.0))
        n_past = total(sel)
        sel_new = jnp.where((key_new > thr) | ((key_new == thr) & (total(eq) < need)), 1.0, 0.0)
        rank_s[...] = flat_rank(sel)
        sel_s[...] = sel

        slot = lax.broadcasted_iota(jnp.int32, (topk, PAGE_SIZE), 0).astype(F32)
        lane = lax.broadcasted_iota(jnp.int32, (topk, PAGE_SIZE), 1)

        def gather_page(pg, acc):
            rk = jnp.broadcast_to(rank_s[pl.ds(pg, 1), :], (topk, PAGE_SIZE))
            sl = jnp.broadcast_to(sel_s[pl.ds(pg, 1), :], (topk, PAGE_SIZE))
            hit = jnp.where(rk == slot, sl, 0.0) > 0.5
            return acc + jnp.where(hit, (pg * PAGE_SIZE + lane).astype(F32), 0.0)

        pos = lax.fori_loop(0, n_pages, gather_page, jnp.zeros((topk, PAGE_SIZE), F32))
        idx_ref[0] = jnp.broadcast_to(jnp.sum(pos, axis=1, keepdims=True), (topk, LANES)).astype(jnp.int32)
        row = lax.broadcasted_iota(jnp.int32, (8, LANES), 0)
        info_ref[0] = jnp.where(row == 0, n_past, sel_new).astype(jnp.int32)


def _idx_sample(page_table, q, w, k_new, cache_k_idx):
    nb, n_pages = page_table.shape
    npg = IDX_PAGES_PER_STEP
    n_steps = n_pages // npg
    topk = min(INDEX_TOPK, (n_pages * PAGE_SIZE + 1) // 4)

    def page_spec(k):
        return pl.BlockSpec((1, PAGE_SIZE, D_IDX), lambda b, j, pt: (pt[b, j * npg + k], 0, 0))

    grid_spec = pltpu.PrefetchScalarGridSpec(
        num_scalar_prefetch=1,
        grid=(nb, n_steps),
        in_specs=[
            pl.BlockSpec((1, H_IDX, D_IDX), lambda b, j, pt: (b, 0, 0)),
            pl.BlockSpec((1, H_IDX, 1), lambda b, j, pt: (b, 0, 0)),
            pl.BlockSpec((1, 1, D_IDX), lambda b, j, pt: (b, 0, 0)),
        ] + [page_spec(k) for k in range(npg)],
        out_specs=[
            pl.BlockSpec((1, topk, LANES), lambda b, j, pt: (b, 0, 0)),
            pl.BlockSpec((1, 8, LANES), lambda b, j, pt: (b, 0, 0)),
        ],
        scratch_shapes=[pltpu.VMEM((n_pages, PAGE_SIZE), F32), pltpu.VMEM((n_pages, PAGE_SIZE), F32),
                        pltpu.VMEM((n_pages, PAGE_SIZE), F32)],
    )
    return pl.pallas_call(
        functools.partial(_idx_sample_kernel, n_steps, n_pages, topk),
        grid_spec=grid_spec,
        out_shape=[jax.ShapeDtypeStruct((nb, topk, LANES), jnp.int32),
                   jax.ShapeDtypeStruct((nb, 8, LANES), jnp.int32)],
        compiler_params=_cparams(("parallel", "arbitrary")),
        name="idx_sample",
    )(page_table, q, w, k_new, *([cache_k_idx] * npg))


def _sp_sample_kernel(topk, rows_ref, npast_ref, selnew_ref, q_ref, knew_ref, vnew_ref, k_hbm, v_hbm,
                      o_ref, kbuf, vbuf, sem):
    b = pl.program_id(0)
    nb = pl.num_programs(0)

    def row_copy(src, dst, slot, seq_b, r):
        row = rows_ref[seq_b * topk + r]
        page = lax.shift_right_logical(row, PAGE_SIZE.bit_length() - 1)
        tok = row & (PAGE_SIZE - 1)
        return pltpu.make_async_copy(src.at[page, 0, pl.ds(tok, 1)], dst.at[slot, pl.ds(r, 1)], sem.at[slot])

    def start_all(seq_b, slot):
        def body(r, _):
            row_copy(k_hbm, kbuf, slot, seq_b, r).start()
            row_copy(v_hbm, vbuf, slot, seq_b, r).start()
            return 0
        lax.fori_loop(0, topk, body, 0)

    def wait_all(seq_b, slot):
        def body(r, _):
            row_copy(k_hbm, kbuf, slot, seq_b, r).wait()
            row_copy(v_hbm, vbuf, slot, seq_b, r).wait()
            return 0
        lax.fori_loop(0, topk, body, 0)

    slot = b % 2

    @pl.when(b == 0)
    def _():
        start_all(0, 0)

    @pl.when(b + 1 < nb)
    def _():
        start_all(b + 1, 1 - slot)

    wait_all(b, slot)

    qbd = _block_diag(q_ref[0], H_SP)
    scale = HEAD_DIM ** -0.5
    ksel = _page_rows(kbuf, H_SP, (slot,)).astype(BF16)
    vsel = _page_rows(vbuf, H_SP, (slot,)).astype(BF16)
    s = _nt_dot(qbd.astype(BF16), ksel) * scale
    lane = lax.broadcasted_iota(jnp.int32, s.shape, 1)
    s = jnp.where(lane < npast_ref[b], s, -jnp.inf)
    s_new = jnp.sum(qbd * knew_ref[0], axis=1, keepdims=True) * scale
    s_new = jnp.where(selnew_ref[b] > 0, s_new, -jnp.inf)
    m = jnp.maximum(jnp.max(s, axis=1, keepdims=True), s_new)
    e = jnp.exp(s - m)
    e_new = jnp.exp(s_new - m)
    denom = jnp.sum(e, axis=1, keepdims=True) + e_new
    acc = _dot(e.astype(BF16), vsel) + e_new * vnew_ref[0]
    o_ref[0] = _diag_rows(acc / denom, H_SP)


def _sp_sample(rows, n_past, sel_new, q, k_new, v_new, cache_k, cache_v, topk):
    nb = q.shape[0]
    vec = pl.BlockSpec((1, 1, W_SP), lambda b, *_: (b, 0, 0))
    grid_spec = pltpu.PrefetchScalarGridSpec(
        num_scalar_prefetch=3,
        grid=(nb,),
        in_specs=[vec, vec, vec, pl.BlockSpec(memory_space=pl.ANY), pl.BlockSpec(memory_space=pl.ANY)],
        out_specs=vec,
        scratch_shapes=[pltpu.VMEM((2, topk, H_SP, HEAD_DIM), F32), pltpu.VMEM((2, topk, H_SP, HEAD_DIM), F32),
                        pltpu.SemaphoreType.DMA((2,))],
    )
    return pl.pallas_call(
        functools.partial(_sp_sample_kernel, topk),
        grid_spec=grid_spec,
        out_shape=jax.ShapeDtypeStruct((nb, 1, W_SP), F32),
        compiler_params=_cparams(("arbitrary",)),
        name="sp_sample",
    )(rows, n_past, sel_new, q, k_new, v_new, cache_k, cache_v)


def _merge_kernel(osb_ref, osp_ref, gsb_ref, gsp_ref, h_ref, wsb_ref, wsp_ref, wout_ref, o_ref, y_s):
    j = pl.program_id(1)

    @pl.when(j == 0)
    def _():
        y_sb = _dot(osb_ref[...].astype(BF16), wsb_ref[...])
        y_sp = _dot(osp_ref[...].astype(BF16), wsp_ref[...])
        y_s[...] = (gsb_ref[...] * y_sb + gsp_ref[...] * y_sp).astype(BF16)

    o_ref[...] = h_ref[...] + _dot(y_s[...], wout_ref[...])


def _merge(o_sb, o_sp, g_sb, g_sp, h, w_sb, w_sp, w_out, n_rows, tm, g_cols):
    tn = 512
    return pl.pallas_call(
        _merge_kernel,
        grid=(n_rows // tm, D_MODEL // tn),
        in_specs=[
            pl.BlockSpec((tm, W_SB), lambda i, j: (i, 0)),
            pl.BlockSpec((tm, W_SP), lambda i, j: (i, 0)),
            pl.BlockSpec((tm, D_MODEL), lambda i, j: (i, g_cols[0])),
            pl.BlockSpec((tm, D_MODEL), lambda i, j: (i, g_cols[1])),
            pl.BlockSpec((tm, tn), lambda i, j: (i, j)),
            pl.BlockSpec((W_SB, D_MODEL), lambda i, j: (0, 0)),
            pl.BlockSpec((W_SP, D_MODEL), lambda i, j: (0, 0)),
            pl.BlockSpec((D_MODEL, tn), lambda i, j: (0, j)),
        ],
        out_specs=pl.BlockSpec((tm, tn), lambda i, j: (i, j)),
        out_shape=jax.ShapeDtypeStruct((n_rows, D_MODEL), F32),
        scratch_shapes=[pltpu.VMEM((tm, D_MODEL), BF16)],
        compiler_params=_cparams(("parallel", "arbitrary")),
        name="merge",
    )(o_sb, o_sp, g_sb, g_sp, h, w_sb, w_sp, w_out)


def kernel(x_prompt, x_sample, cache_k_sb, cache_v_sb, cache_k_sp, cache_v_sp, cache_k_idx, page_table, meta_tokens, g_ffn1, w_ffn1_gate, w_ffn1_up, w_ffn1_down, g_mix, w_in, w_proj_sb, w_proj_sp, w_out, g_ffn2, w_ffn2_gate, w_ffn2_up, w_ffn2_down, g_final):
    batch, seq, _ = x_prompt.shape
    nb, dec_seq, _ = x_sample.shape
    n_pool = cache_k_sb.shape[0]
    n_pages = page_table.shape[1]
    past_len = n_pages * PAGE_SIZE
    assert batch == 1 and dec_seq == 1 and g_ffn1.shape[0] == 1
    assert seq % ROW_TILE == 0 and N_META + nb <= ROW_TILE
    n_rows = seq + ROW_TILE
    samp0 = seq + N_META

    h0 = jnp.concatenate([x_prompt[0], meta_tokens.astype(F32), x_sample[:, 0, :],
                          jnp.zeros((n_rows - samp0 - nb, D_MODEL), F32)], axis=0)
    pos = jnp.concatenate([N_META + jnp.arange(seq, dtype=jnp.int32), jnp.arange(N_META, dtype=jnp.int32),
                           jnp.full((nb,), past_len, jnp.int32), jnp.zeros((n_rows - samp0 - nb,), jnp.int32)])
    tables = _rope_tables(pos)
    wi = w_in[0]
    o_k_idx = 7168
    o_g_sb = o_k_idx + D_IDX + H_IDX
    w_proj = jnp.concatenate([
        wi[:, o_g_sb:],
        wi[:, 0:3 * W_SB],
        wi[:, 3 * W_SB + 2 * W_SP:3 * W_SB + 3 * W_SP],
        wi[:, 3 * W_SB:3 * W_SB + 2 * W_SP],
        wi[:, 3 * W_SB + 3 * W_SP:o_k_idx],
        wi[:, o_k_idx:o_g_sb], jnp.zeros((D_MODEL, PROJ_TILE - D_IDX - H_IDX), F32),
    ], axis=1).astype(BF16)
    bf = lambda w: w[0].astype(BF16)
    row = lambda g: g.reshape(1, D_MODEL)

    h1 = _ffn(h0, n_rows, ROW_TILE, row(g_ffn1[0]), bf(w_ffn1_gate), bf(w_ffn1_up), bf(w_ffn1_down))
    p = _project(h1, row(g_mix[0]), w_proj, tables, n_rows)

    o_sb_p = _sb_prompt(p, seq, n_rows)
    bias = _idx_prompt(p, seq, n_rows)
    o_sp_p = _sp_prompt(p, bias, seq, n_rows)

    ps = p[samp0:samp0 + nb]
    seg = lambda base, w: ps[:, base:base + w]
    o_sb_s = _sb_sample(page_table, seg(P_Q_SB, W_SB).reshape(nb, 1, W_SB), cache_k_sb, cache_v_sb)
    idx, info = _idx_sample(page_table, seg(P_Q_IDX, H_IDX * D_IDX).reshape(nb, H_IDX, D_IDX),
                            seg(P_SMALL + W_IDX_LANE, H_IDX).reshape(nb, H_IDX, 1),
                            seg(P_SMALL, D_IDX).reshape(nb, 1, D_IDX),
                            cache_k_idx.reshape(n_pool, PAGE_SIZE, D_IDX))
    topk = idx.shape[1]
    idx = idx[:, :, 0]
    rows = jnp.take_along_axis(page_table, idx // PAGE_SIZE, axis=1) * PAGE_SIZE + idx % PAGE_SIZE
    o_sp_s = _sp_sample(rows.reshape(-1), info[:, 0, 0], info[:, 1, 0],
                        seg(P_Q_SP, W_SP).reshape(nb, 1, W_SP), seg(P_K_SP, W_SP).reshape(nb, 1, W_SP),
                        seg(P_V_SP, W_SP).reshape(nb, 1, W_SP), cache_k_sp, cache_v_sp, topk)

    wsb, wsp, wo = bf(w_proj_sb), bf(w_proj_sp), bf(w_out)
    ffn2 = (row(g_ffn2[0]), bf(w_ffn2_gate), bf(w_ffn2_up), bf(w_ffn2_down), row(g_final))
    h2_p = _merge(o_sb_p, o_sp_p, p, p, h1, wsb, wsp, wo, seq, ROW_TILE, (P_G_SB // D_MODEL, P_G_SP // D_MODEL))
    y_p = _ffn(h2_p, seq, ROW_TILE, *ffn2)
    h2_s = _merge(o_sb_s.reshape(nb, W_SB), o_sp_s.reshape(nb, W_SP), seg(P_G_SB, D_MODEL), seg(P_G_SP, D_MODEL),
                  h1[samp0:samp0 + nb], wsb, wsp, wo, nb, nb, (0, 0))
    y_s = _ffn(h2_s, nb, nb, *ffn2)

    def prompt_rows(base, w):
        return jnp.concatenate([p[seq:seq + N_META, base:base + w], p[:seq, base:base + w]], axis=0)

    def kv_p(base, nh):
        return prompt_rows(base, nh * HEAD_DIM).reshape(1, 1, seq + N_META, nh, HEAD_DIM)

    def kv_s(base, nh):
        return seg(base, nh * HEAD_DIM).reshape(nb, 1, 1, nh, HEAD_DIM)

    return (y_p.reshape(1, seq, D_MODEL), y_s.reshape(nb, 1, D_MODEL),
            kv_p(P_K_SB, H_SB), kv_p(P_V_SB, H_SB), kv_p(P_K_SP, H_SP), kv_p(P_V_SP, H_SP),
            prompt_rows(P_SMALL, D_IDX).reshape(1, 1, seq + N_META, D_IDX),
            kv_s(P_K_SB, H_SB), kv_s(P_V_SB, H_SB), kv_s(P_K_SP, H_SP), kv_s(P_V_SP, H_SP),
            seg(P_SMALL, D_IDX).reshape(nb, 1, 1, D_IDX))
```

```python
import functools

import jax
import jax.numpy as jnp
from jax import lax
from jax.experimental import pallas as pl
from jax.experimental.pallas import tpu as pltpu

D_MODEL = 2048
N_META = 16
HEAD_DIM = 128
H_SB = 8
H_SP = 8
W_SB = H_SB * HEAD_DIM
W_SP = H_SP * HEAD_DIM
H_IDX = 16
D_IDX = 64
INDEX_TOPK = 256
D_FF = 5632
ROPE_THETA = 500000.0
EPS = 1e-6
PAGE_SIZE = 128

LANES = 128
VMEM_LIMIT_BYTES = 56 * 1024 * 1024

ROW_TILE = 512
FF_TILE = 512
PROJ_TILE = 512
QBLK = 256
KEY_TILE = 512
TRI = 256
SB_QBLK = 512
HEAD_GROUP = 2
COUNT_ROWS = 128
META_TILE = 128
SB_PAGES_PER_STEP = 8
IDX_PAGES_PER_STEP = 16

P_G_SB = 0
P_G_SP = P_G_SB + D_MODEL
P_Q_SB = P_G_SP + D_MODEL
P_K_SB = P_Q_SB + W_SB
P_V_SB = P_K_SB + W_SB
P_V_SP = P_V_SB + W_SB
P_Q_SP = P_V_SP + W_SP
P_K_SP = P_Q_SP + W_SP
P_Q_IDX = P_K_SP + W_SP
P_SMALL = P_Q_IDX + H_IDX * D_IDX
P_COLS = P_SMALL + PROJ_TILE
W_IDX_LANE = D_IDX

_J_PLAIN = P_Q_SB // PROJ_TILE
_J_ROPE128 = P_Q_SP // PROJ_TILE
_J_ROPE64 = P_Q_IDX // PROJ_TILE
_J_SMALL = P_SMALL // PROJ_TILE

INT_MIN = -(2 ** 31)
F32 = jnp.float32
BF16 = jnp.bfloat16


def _cparams(sem):
    return pltpu.CompilerParams(dimension_semantics=sem, vmem_limit_bytes=VMEM_LIMIT_BYTES)


def _rms(x):
    return x * lax.rsqrt(jnp.mean(x * x, axis=-1, keepdims=True) + EPS)


def _nt_dot(a, b):
    return lax.dot_general(a, b, (((1,), (1,)), ((), ())), preferred_element_type=F32)


def _dot(a, b):
    return jnp.dot(a, b, preferred_element_type=F32)


def _ffn_kernel(final, n_ff, h_ref, g_ref, wg_ref, wu_ref, wd_ref, *rest):
    if final:
        gf_ref, o_ref, u_s, acc_s = rest
    else:
        o_ref, u_s, acc_s = rest
    j = pl.program_id(1)

    @pl.when(j == 0)
    def _():
        u_s[...] = (_rms(h_ref[...]) * g_ref[...]).astype(BF16)
        acc_s[...] = jnp.zeros_like(acc_s)

    u = u_s[...]
    a = _dot(u, wg_ref[...])
    b = _dot(u, wu_ref[...])
    act = (a * jax.nn.sigmoid(a)) * b
    acc_s[...] += _dot(act.astype(BF16), wd_ref[...])

    @pl.when(j == n_ff - 1)
    def _():
        r = h_ref[...] + 0.5 * acc_s[...]
        if final:
            r = _rms(r) * gf_ref[...]
        o_ref[...] = r


def _ffn(h, n_rows, tm, gain, wg, wu, wd, g_final=None):
    final = g_final is not None
    n_ff = D_FF // FF_TILE
    in_specs = [
        pl.BlockSpec((tm, D_MODEL), lambda i, j: (i, 0)),
        pl.BlockSpec((1, D_MODEL), lambda i, j: (0, 0)),
        pl.BlockSpec((D_MODEL, FF_TILE), lambda i, j: (0, j)),
        pl.BlockSpec((D_MODEL, FF_TILE), lambda i, j: (0, j)),
        pl.BlockSpec((FF_TILE, D_MODEL), lambda i, j: (j, 0)),
    ]
    args = [h, gain, wg, wu, wd]
    if final:
        in_specs.append(pl.BlockSpec((1, D_MODEL), lambda i, j: (0, 0)))
        args.append(g_final)
    return pl.pallas_call(
        functools.partial(_ffn_kernel, final, n_ff),
        grid=(n_rows // tm, n_ff),
        in_specs=in_specs,
        out_specs=pl.BlockSpec((tm, D_MODEL), lambda i, j: (i, 0)),
        out_shape=jax.ShapeDtypeStruct((n_rows, D_MODEL), F32),
        scratch_shapes=[pltpu.VMEM((tm, D_MODEL), BF16), pltpu.VMEM((tm, D_MODEL), F32)],
        compiler_params=_cparams(("parallel", "arbitrary")),
        name="ffn_final" if final else "ffn",
    )(*args)


def _rotary(x, c, s1, s2, shift):
    return x * c + pltpu.roll(x, LANES - shift, 1) * s1 + pltpu.roll(x, shift, 1) * s2


def _proj_kernel(h_ref, g_ref, w_ref, t_ref, o_ref, u_s):
    j = pl.program_id(1)

    @pl.when(j == 0)
    def _():
        u_s[...] = (_rms(h_ref[...]) * g_ref[...]).astype(BF16)

    def tables(k):
        return [t_ref[:, (3 * k + m) * LANES:(3 * k + m + 1) * LANES] for m in range(3)]

    def rotary_tiles(y, k, shift, n_tiles):
        c, s1, s2 = tables(k)
        for t in range(PROJ_TILE // LANES):
            x = y[:, t * LANES:(t + 1) * LANES]
            if t < n_tiles:
                x = _rotary(x, c, s1, s2, shift)
            o_ref[:, t * LANES:(t + 1) * LANES] = x

    is_gate = j < _J_PLAIN
    is_plain = (j >= _J_PLAIN) & (j < _J_ROPE128)
    is_r128 = (j >= _J_ROPE128) & (j < _J_ROPE64)
    is_r64 = (j >= _J_ROPE64) & (j < _J_SMALL)
    is_small = j == _J_SMALL

    @pl.when(is_plain)
    def _():
        o_ref[...] = _dot(u_s[...], w_ref[...])

    @pl.when(is_r128)
    def _():
        rotary_tiles(_dot(u_s[...], w_ref[...]), 0, HEAD_DIM // 8, PROJ_TILE // LANES)

    @pl.when(is_r64)
    def _():
        rotary_tiles(_dot(u_s[...], w_ref[...]), 1, D_IDX // 8, PROJ_TILE // LANES)

    @pl.when(is_small)
    def _():
        rotary_tiles(_dot(u_s[...], w_ref[...]), 2, D_IDX // 8, 1)

    @pl.when(is_gate)
    def _():
        o_ref[...] = jax.nn.sigmoid(_dot(u_s[...], w_ref[...]))


def _project(h, gain, w_proj, tables, n_rows):
    return pl.pallas_call(
        _proj_kernel,
        grid=(n_rows // ROW_TILE, P_COLS // PROJ_TILE),
        in_specs=[
            pl.BlockSpec((ROW_TILE, D_MODEL), lambda i, j: (i, 0)),
            pl.BlockSpec((1, D_MODEL), lambda i, j: (0, 0)),
            pl.BlockSpec((D_MODEL, PROJ_TILE), lambda i, j: (0, j)),
            pl.BlockSpec((ROW_TILE, 9 * LANES), lambda i, j: (i, 0)),
        ],
        out_specs=pl.BlockSpec((ROW_TILE, PROJ_TILE), lambda i, j: (i, j)),
        out_shape=jax.ShapeDtypeStruct((n_rows, P_COLS), F32),
        scratch_shapes=[pltpu.VMEM((ROW_TILE, D_MODEL), BF16)],
        compiler_params=_cparams(("parallel", "arbitrary")),
        name="project",
    )(h, gain, w_proj, tables)


def _rope_tables(pos):
    n = pos.shape[0]
    posf = pos.astype(F32)

    def cs(half):
        inv_freq = ROPE_THETA ** (-jnp.arange(half, dtype=F32) / half)
        ang = posf[:, None] * inv_freq[None, :]
        return jnp.cos(ang), jnp.sin(ang)

    def z(w):
        return jnp.zeros((n, w), F32)

    def o(w):
        return jnp.ones((n, w), F32)

    c16, s16 = cs(HEAD_DIM // 8)
    c8, s8 = cs(D_IDX // 8)
    t128 = [jnp.concatenate([c16, c16, o(96)], 1), jnp.concatenate([-s16, z(112)], 1),
            jnp.concatenate([z(16), s16, z(96)], 1)]
    t64 = [jnp.tile(jnp.concatenate([c8, c8, o(48)], 1), (1, 2)), jnp.tile(jnp.concatenate([-s8, z(56)], 1), (1, 2)),
           jnp.tile(jnp.concatenate([z(8), s8, z(48)], 1), (1, 2))]
    tsm = [jnp.concatenate([c8, c8, o(48), (H_IDX ** -0.5) * o(16), o(48)], 1), jnp.concatenate([-s8, z(120)], 1),
           jnp.concatenate([z(8), s8, z(112)], 1)]
    return jnp.concatenate(t128 + t64 + tsm, axis=1)


def _strict_lower_ones(n):
    r = lax.broadcasted_iota(jnp.int32, (n, n), 0)
    c = lax.broadcasted_iota(jnp.int32, (n, n), 1)
    return jnp.where(r > c, 1.0, 0.0).astype(BF16)


def _sb_logs(z):
    log_beta = jnp.minimum(z, 0.0) - jnp.log(1.0 + jnp.exp(-jnp.abs(z)))
    return log_beta, log_beta - z


def _suffix_sums(x, tri):
    m, n = x.shape
    t = tri.shape[0]
    nsub = n // t
    hi = x.astype(BF16)
    lo = (x - hi.astype(F32)).astype(BF16)
    parts = []
    for c in range(nsub):
        parts += [hi[:, c * t:(c + 1) * t], lo[:, c * t:(c + 1) * t]]
    res = _dot(jnp.concatenate(parts, axis=0), tri)
    outs = []
    tail = None
    for c in reversed(range(nsub)):
        s = res[2 * c * m:(2 * c + 1) * m] + res[(2 * c + 1) * m:(2 * c + 2) * m]
        total = jnp.sum(x[:, c * t:(c + 1) * t], axis=1, keepdims=True)
        if tail is not None:
            s = s + tail
            total = total + tail
        outs.append(s)
        tail = total
    return jnp.concatenate(outs[::-1], axis=1), tail


def _sb_tile(q, kt, vt, tri, rsum, acc, valid):
    log_beta, log_keep = _sb_logs(_nt_dot(q, kt) * (HEAD_DIM ** -0.5))
    if valid is not None:
        log_keep = jnp.where(valid, log_keep, 0.0)
    after, total = _suffix_sums(log_keep, tri)
    a = jnp.exp(log_beta + (after + rsum))
    if valid is not None:
        a = jnp.where(valid, a, 0.0)
    return rsum + total, acc + _dot(a.astype(BF16), vt)


def _sb_prompt_kernel(seq, q_ref, k_ref, v_ref, tri_ref, o_ref):
    i = pl.program_id(1)
    tk = KEY_TILE
    qb = SB_QBLK
    tri = tri_ref[...]
    q_row = i * qb + lax.broadcasted_iota(jnp.int32, (qb, tk), 0)
    lane = lax.broadcasted_iota(jnp.int32, (qb, tk), 1)
    heads = [slice(g * HEAD_DIM, (g + 1) * HEAD_DIM) for g in range(HEAD_GROUP)]
    qs = [q_ref[:, hs].astype(BF16) for hs in heads]

    def tile(start, carry, valid, width=tk, tri=tri):
        out = []
        for g, hs in enumerate(heads):
            kt = k_ref[pl.ds(start, width), hs].astype(BF16)
            vt = v_ref[pl.ds(start, width), hs].astype(BF16)
            out.append(_sb_tile(qs[g], kt, vt, tri, carry[g][0], carry[g][1], valid))
        return tuple(out)

    carry = tuple((jnp.zeros((qb, 1), F32), jnp.zeros((qb, HEAD_DIM), F32)) for _ in heads)

    c_diag = (i * qb) // tk
    d_start = pl.multiple_of(c_diag * tk, tk)
    carry = tile(d_start, carry, (d_start + lane) < q_row)
    carry = lax.fori_loop(0, c_diag, lambda it, cr: tile(pl.multiple_of((c_diag - 1 - it) * tk, tk), cr, None), carry)
    meta_valid = lax.broadcasted_iota(jnp.int32, (qb, META_TILE), 1) < N_META
    carry = tile(seq, carry, meta_valid, META_TILE, tri_ref[:META_TILE, :META_TILE])
    for g, hs in enumerate(heads):
        o_ref[:, hs] = carry[g][1]


def _sb_prompt(p, seq, n_rows):
    gw = HEAD_GROUP * HEAD_DIM
    tri = jnp.tril(jnp.ones((TRI, TRI), F32), -1).astype(BF16)
    return pl.pallas_call(
        functools.partial(_sb_prompt_kernel, seq),
        grid=(H_SB // HEAD_GROUP, seq // SB_QBLK),
        in_specs=[
            pl.BlockSpec((SB_QBLK, gw), lambda h, i: (i, P_Q_SB // gw + h)),
            pl.BlockSpec((n_rows, gw), lambda h, i: (0, P_K_SB // gw + h)),
            pl.BlockSpec((n_rows, gw), lambda h, i: (0, P_V_SB // gw + h)),
            pl.BlockSpec((TRI, TRI), lambda h, i: (0, 0)),
        ],
        out_specs=pl.BlockSpec((SB_QBLK, gw), lambda h, i: (i, h)),
        out_shape=jax.ShapeDtypeStruct((seq, W_SB), F32),
        compiler_params=_cparams(("parallel", "arbitrary")),
        name="sb_prompt",
    )(p, p, p, tri)


def _sortable_key(x):
    b = lax.bitcast_convert_type(x, jnp.int32)
    return jnp.where(b < 0, b ^ jnp.int32(0x7FFFFFFF), b)


def _idx_scores(qidx, wcols, ksmall):
    m = qidx.shape[0]
    n_pairs = H_IDX // 2
    lane = lax.broadcasted_iota(jnp.int32, ksmall.shape, 1)
    k_even = jnp.where(lane < D_IDX, ksmall, 0.0)
    k_odd = pltpu.roll(k_even, D_IDX, 1)
    q_stack = jnp.concatenate([qidx[:, hp * LANES:(hp + 1) * LANES] for hp in range(n_pairs)], axis=0)
    s_par = [_nt_dot(q_stack, kk.astype(BF16)) for kk in (k_even, k_odd)]
    acc = None
    for hp in range(n_pairs):
        for par in range(2):
            term = wcols[2 * hp + par] * jnp.maximum(s_par[par][hp * m:(hp + 1) * m], 0.0)
            acc = term if acc is None else acc + term
    return acc


def _topk_threshold(count_ge, shape, topk):
    def body(b, p):
        cand = p + lax.shift_left(jnp.int32(1), jnp.int32(31) - b)
        return jnp.where(count_ge(cand) >= topk, cand, p)

    return lax.fori_loop(0, 32, body, jnp.full(shape, INT_MIN, jnp.int32))


def _idx_prompt_kernel(seq, topk, qidx_ref, qsmall_ref, ksmall_ref, bias_ref, keys_s):
    i = pl.program_id(0)
    tk = KEY_TILE
    n_own = (i * QBLK + QBLK - 1) // tk + 1
    meta_start = seq
    qidx = qidx_ref[...].astype(BF16)
    qsmall = qsmall_ref[...] * (D_IDX ** -0.5)
    wcols = [qsmall[:, W_IDX_LANE + h:W_IDX_LANE + h + 1] for h in range(H_IDX)]
    q_row = i * QBLK + lax.broadcasted_iota(jnp.int32, (QBLK, tk), 0)
    lane = lax.broadcasted_iota(jnp.int32, (QBLK, tk), 1)
    meta_valid = lax.broadcasted_iota(jnp.int32, (QBLK, META_TILE), 1) < N_META

    def fill(start, valid, width):
        sc = _idx_scores(qidx, wcols, ksmall_ref[pl.ds(start, width), :])
        keys_s[:, pl.ds(start, width)] = jnp.where(valid, _sortable_key(sc), INT_MIN)

    def fill_body(c, _):
        start = pl.multiple_of(c * tk, tk)
        fill(start, (start + lane) <= q_row, tk)
        return 0

    lax.fori_loop(0, n_own, fill_body, 0)
    fill(meta_start, meta_valid, META_TILE)

    def count_ge(cand):
        counts = []
        for r0 in range(0, QBLK, COUNT_ROWS):
            cb = jnp.broadcast_to(cand[r0:r0 + COUNT_ROWS], (COUNT_ROWS, LANES))

            def add_tile(start, cnt, width, r0=r0, cb=cb):
                kt = keys_s[pl.ds(r0, COUNT_ROWS), pl.ds(start, width)]
                for t in range(width // LANES):
                    cnt = cnt + jnp.where(kt[:, t * LANES:(t + 1) * LANES] >= cb, 1.0, 0.0)
                return cnt

            cnt = lax.fori_loop(0, n_own, lambda c, cnt: add_tile(pl.multiple_of(c * tk, tk), cnt, tk),
                                jnp.zeros((COUNT_ROWS, LANES), F32))
            cnt = add_tile(meta_start, cnt, META_TILE)
            counts.append(jnp.sum(cnt, axis=1, keepdims=True))
        return jnp.concatenate(counts, axis=0)

    thr = _topk_threshold(count_ge, (QBLK, 1), float(topk))

    bias_ref[...] = jnp.full(bias_ref.shape, -jnp.inf, BF16)

    def emit(start, width):
        kt = keys_s[:, pl.ds(start, width)]
        sel = (kt >= thr) & (kt != INT_MIN)
        bias_ref[:, pl.ds(start, width)] = jnp.where(sel, 0.0, -jnp.inf).astype(BF16)

    def emit_body(c, _):
        emit(pl.multiple_of(c * tk, tk), tk)
        return 0

    lax.fori_loop(0, n_own, emit_body, 0)
    emit(meta_start, META_TILE)


def _idx_prompt(p, seq, n_rows):
    nq = seq // QBLK
    topk = min(INDEX_TOPK, (seq + N_META) // 4)
    return pl.pallas_call(
        functools.partial(_idx_prompt_kernel, seq, topk),
        grid=(nq,),
        in_specs=[
            pl.BlockSpec((QBLK, H_IDX * D_IDX), lambda i: (i, P_Q_IDX // (H_IDX * D_IDX))),
            pl.BlockSpec((QBLK, LANES), lambda i: (i, P_SMALL // LANES)),
            pl.BlockSpec((n_rows, LANES), lambda i: (0, P_SMALL // LANES)),
        ],
        out_specs=pl.BlockSpec((QBLK, n_rows), lambda i: (i, 0)),
        out_shape=jax.ShapeDtypeStruct((seq, n_rows), BF16),
        scratch_shapes=[pltpu.VMEM((QBLK, n_rows), jnp.int32)],
        compiler_params=_cparams(("parallel",)),
        name="idx_prompt",
    )(p, p, p)


def _sp_prompt_kernel(seq, q_ref, k_ref, v_ref, bias_ref, o_ref):
    i = pl.program_id(1)
    tk = KEY_TILE
    n_own = (i * QBLK + QBLK - 1) // tk + 1
    heads = [slice(g * HEAD_DIM, (g + 1) * HEAD_DIM) for g in range(HEAD_GROUP)]
    qs = [q_ref[:, hs].astype(BF16) for hs in heads]

    def tile(start, carry, width=tk):
        bias = bias_ref[:, pl.ds(start, width)].astype(F32)
        out = []
        for g, hs in enumerate(heads):
            m, l, acc = carry[g]
            kt = k_ref[pl.ds(start, width), hs].astype(BF16)
            vt = v_ref[pl.ds(start, width), hs].astype(BF16)
            s = _nt_dot(qs[g], kt) * (HEAD_DIM ** -0.5) + bias
            m_new = jnp.maximum(m, jnp.max(s, axis=1, keepdims=True))
            alpha = jnp.exp(m - m_new)
            e = jnp.exp(s - m_new)
            l = alpha * l + jnp.sum(e, axis=1, keepdims=True)
            acc = alpha * acc + _dot(e.astype(BF16), vt)
            out.append((m_new, l, acc))
        return tuple(out)

    carry = tuple((jnp.full((QBLK, 1), -1e30, F32), jnp.zeros((QBLK, 1), F32), jnp.zeros((QBLK, HEAD_DIM), F32))
                  for _ in heads)
    carry = tile(seq, carry, META_TILE)
    carry = lax.fori_loop(0, n_own, lambda c, cr: tile(pl.multiple_of(c * tk, tk), cr), carry)
    for g, hs in enumerate(heads):
        o_ref[:, hs] = carry[g][2] / carry[g][1]


def _sp_prompt(p, bias, seq, n_rows):
    gw = HEAD_GROUP * HEAD_DIM
    return pl.pallas_call(
        functools.partial(_sp_prompt_kernel, seq),
        grid=(H_SP // HEAD_GROUP, seq // QBLK),
        in_specs=[
            pl.BlockSpec((QBLK, gw), lambda h, i: (i, P_Q_SP // gw + h)),
            pl.BlockSpec((n_rows, gw), lambda h, i: (0, P_K_SP // gw + h)),
            pl.BlockSpec((n_rows, gw), lambda h, i: (0, P_V_SP // gw + h)),
            pl.BlockSpec((QBLK, n_rows), lambda h, i: (i, 0)),
        ],
        out_specs=pl.BlockSpec((QBLK, gw), lambda h, i: (i, h)),
        out_shape=jax.ShapeDtypeStruct((seq, W_SP), F32),
        compiler_params=_cparams(("parallel", "arbitrary")),
        name="sp_prompt",
    )(p, p, p, bias)


Q_ROWS = 16


def _block_diag(qrow, n_heads):
    w = n_heads * HEAD_DIM
    r = lax.broadcasted_iota(jnp.int32, (Q_ROWS, w), 0)
    c = lax.broadcasted_iota(jnp.int32, (Q_ROWS, w), 1)
    return jnp.where((c // HEAD_DIM) == r, jnp.broadcast_to(qrow, (Q_ROWS, w)), 0.0)


def _diag_rows(x, n_heads):
    w = n_heads * HEAD_DIM
    r = lax.broadcasted_iota(jnp.int32, (Q_ROWS, w), 0)
    c = lax.broadcasted_iota(jnp.int32, (Q_ROWS, w), 1)
    return jnp.sum(jnp.where((c // HEAD_DIM) == r, x, 0.0), axis=0, keepdims=True)


def _page_rows(ref, n_heads, lead=(0, 0)):
    r = ref.at[lead]
    t = r.shape[0]
    flat = r.reshape(t * n_heads, HEAD_DIM)
    return jnp.concatenate([flat[pl.ds(h, t, stride=n_heads), :] for h in range(n_heads)], axis=1)


def _sb_sample_kernel(n_steps, pt_ref, q_ref, *rest):
    npg = SB_PAGES_PER_STEP
    k_refs = rest[:npg]
    v_refs = rest[npg:2 * npg]
    o_ref, rsum_s, acc_s = rest[2 * npg:]
    j = pl.program_id(1)

    @pl.when(j == 0)
    def _():
        rsum_s[...] = jnp.zeros_like(rsum_s)
        acc_s[...] = jnp.zeros_like(acc_s)

    q = _block_diag(q_ref[0], H_SB).astype(BF16)
    tri = _strict_lower_ones(PAGE_SIZE)
    z = jnp.concatenate([_nt_dot(q, _page_rows(k_refs[k], H_SB).astype(BF16)) for k in range(npg)], axis=0)
    log_beta, log_keep = _sb_logs(z * (HEAD_DIM ** -0.5))
    after, _ = _suffix_sums(log_keep, tri)
    totals = jnp.sum(log_keep, axis=1, keepdims=True)
    rsum = rsum_s[...]
    carried = []
    for k in range(npg):
        carried.append(rsum)
        rsum = rsum + totals[k * Q_ROWS:(k + 1) * Q_ROWS]
    rsum_s[...] = rsum
    a = jnp.exp(log_beta + (after + jnp.concatenate(carried, axis=0))).astype(BF16)
    acc = acc_s[...]
    for k in range(npg):
        acc = acc + _dot(a[k * Q_ROWS:(k + 1) * Q_ROWS], _page_rows(v_refs[k], H_SB).astype(BF16))
    acc_s[...] = acc

    @pl.when(j == n_steps - 1)
    def _():
        o_ref[0] = _diag_rows(acc, H_SB)


def _sb_sample(page_table, q, cache_k, cache_v):
    nb, n_pages = page_table.shape
    npg = SB_PAGES_PER_STEP
    n_steps = n_pages // npg

    def page_spec(k):
        return pl.BlockSpec((1, 1, PAGE_SIZE, H_SB, HEAD_DIM),
                            lambda b, j, pt: (pt[b, n_pages - 1 - (j * npg + k)], 0, 0, 0, 0))

    grid_spec = pltpu.PrefetchScalarGridSpec(
        num_scalar_prefetch=1,
        grid=(nb, n_steps),
        in_specs=[pl.BlockSpec((1, 1, W_SB), lambda b, j, pt: (b, 0, 0))]
        + [page_spec(k) for k in range(npg)] + [page_spec(k) for k in range(npg)],
        out_specs=pl.BlockSpec((1, 1, W_SB), lambda b, j, pt: (b, 0, 0)),
        scratch_shapes=[pltpu.VMEM((Q_ROWS, 1), F32), pltpu.VMEM((Q_ROWS, W_SB), F32)],
    )
    return pl.pallas_call(
        functools.partial(_sb_sample_kernel, n_steps),
        grid_spec=grid_spec,
        out_shape=jax.ShapeDtypeStruct((nb, 1, W_SB), F32),
        compiler_params=_cparams(("parallel", "arbitrary")),
        name="sb_sample",
    )(page_table, q, *([cache_k] * npg), *([cache_v] * npg))


def _idx_sample_kernel(n_steps, n_pages, topk, pt_ref, q_ref, w_ref, knew_ref, *rest):
    npg = IDX_PAGES_PER_STEP
    k_refs = rest[:npg]
    idx_ref, info_ref, sc_s, rank_s, sel_s = rest[npg:]
    j = pl.program_id(1)
    q = q_ref[0].astype(BF16)
    w = w_ref[0]

    def score(kpt):
        s = _dot(q, kpt.astype(BF16)) * (D_IDX ** -0.5)
        return jnp.sum(w * jnp.maximum(s, 0.0), axis=0, keepdims=True)

    for k in range(npg):
        sc_s[pl.ds(j * npg + k, 1), :] = score(k_refs[k][0])

    @pl.when(j == n_steps - 1)
    def _():
        keys = _sortable_key(sc_s[...])
        s_new = jnp.sum(q.astype(F32) * knew_ref[0].astype(BF16).astype(F32), axis=1, keepdims=True)
        sc_new = jnp.sum(w * jnp.maximum(s_new * (D_IDX ** -0.5), 0.0), axis=0, keepdims=True)
        key_new = _sortable_key(sc_new)

        def total(x):
            return jnp.sum(jnp.sum(x, axis=1, keepdims=True), axis=0, keepdims=True)

        def count_ge(cand):
            return total(jnp.where(keys >= cand, 1.0, 0.0)) + jnp.where(key_new >= cand, 1.0, 0.0)

        thr = _topk_threshold(count_ge, (1, 1), float(topk))
        gt = jnp.where(keys > thr, 1.0, 0.0)
        eq = jnp.where(keys == thr, 1.0, 0.0)
        n_gt = total(gt) + jnp.where(key_new > thr, 1.0, 0.0)
        need = float(topk) - n_gt

        r = lax.broadcasted_iota(jnp.int32, (n_pages, n_pages), 0)
        c = lax.broadcasted_iota(jnp.int32, (n_pages, n_pages), 1)
        rows_before = jnp.where(c < r, 1.0, 0.0).astype(BF16)
        rl = lax.broadcasted_iota(jnp.int32, (PAGE_SIZE, PAGE_SIZE), 0)
        cl = lax.broadcasted_iota(jnp.int32, (PAGE_SIZE, PAGE_SIZE), 1)
        lanes_before = jnp.where(rl < cl, 1.0, 0.0).astype(BF16)

        def flat_rank(x):
            row_cnt = jnp.broadcast_to(jnp.sum(x, axis=1, keepdims=True), x.shape)
            return _dot(rows_before, row_cnt.astype(BF16)) + _dot(x.astype(BF16), lanes_before)

        sel = jnp.maximum(gt, jnp.where(flat_rank(eq) < need, eq, 0.0))
        n_past = total(sel)
        sel_new = jnp.where((key_new > thr) | ((key_new == thr) & (total(eq) < need)), 1.0, 0.0)
        rank_s[...] = flat_rank(sel)
        sel_s[...] = sel

        slot = lax.broadcasted_iota(jnp.int32, (topk, PAGE_SIZE), 0).astype(F32)
        lane = lax.broadcasted_iota(jnp.int32, (topk, PAGE_SIZE), 1)

        def gather_page(pg, acc):
            rk = jnp.broadcast_to(rank_s[pl.ds(pg, 1), :], (topk, PAGE_SIZE))
            sl = jnp.broadcast_to(sel_s[pl.ds(pg, 1), :], (topk, PAGE_SIZE))
            hit = jnp.where(rk == slot, sl, 0.0) > 0.5
            return acc + jnp.where(hit, (pg * PAGE_SIZE + lane).astype(F32), 0.0)

        pos = lax.fori_loop(0, n_pages, gather_page, jnp.zeros((topk, PAGE_SIZE), F32))
        idx_ref[0] = jnp.broadcast_to(jnp.sum(pos, axis=1, keepdims=True), (topk, LANES)).astype(jnp.int32)
        row = lax.broadcasted_iota(jnp.int32, (8, LANES), 0)
        info_ref[0] = jnp.where(row == 0, n_past, sel_new).astype(jnp.int32)


def _idx_sample(page_table, q, w, k_new, cache_k_idx):
    nb, n_pages = page_table.shape
    npg = IDX_PAGES_PER_STEP
    n_steps = n_pages // npg
    topk = min(INDEX_TOPK, (n_pages * PAGE_SIZE + 1) // 4)

    def page_spec(k):
        return pl.BlockSpec((1, D_IDX, PAGE_SIZE), lambda b, j, pt: (pt[b, j * npg + k], 0, 0))

    grid_spec = pltpu.PrefetchScalarGridSpec(
        num_scalar_prefetch=1,
        grid=(nb, n_steps),
        in_specs=[
            pl.BlockSpec((1, H_IDX, D_IDX), lambda b, j, pt: (b, 0, 0)),
            pl.BlockSpec((1, H_IDX, 1), lambda b, j, pt: (b, 0, 0)),
            pl.BlockSpec((1, 1, D_IDX), lambda b, j, pt: (b, 0, 0)),
        ] + [page_spec(k) for k in range(npg)],
        out_specs=[
            pl.BlockSpec((1, topk, LANES), lambda b, j, pt: (b, 0, 0)),
            pl.BlockSpec((1, 8, LANES), lambda b, j, pt: (b, 0, 0)),
        ],
        scratch_shapes=[pltpu.VMEM((n_pages, PAGE_SIZE), F32), pltpu.VMEM((n_pages, PAGE_SIZE), F32),
                        pltpu.VMEM((n_pages, PAGE_SIZE), F32)],
    )
    return pl.pallas_call(
        functools.partial(_idx_sample_kernel, n_steps, n_pages, topk),
        grid_spec=grid_spec,
        out_shape=[jax.ShapeDtypeStruct((nb, topk, LANES), jnp.int32),
                   jax.ShapeDtypeStruct((nb, 8, LANES), jnp.int32)],
        compiler_params=_cparams(("parallel", "arbitrary")),
        name="idx_sample",
    )(page_table, q, w, k_new, *([cache_k_idx] * npg))


def _sp_sample_kernel(topk, n_pages, pt_ref, idx_ref, npast_ref, selnew_ref, q_ref, knew_ref, vnew_ref,
                      k_hbm, v_hbm, o_ref, kbuf, vbuf, sem):
    b = pl.program_id(0)
    nb = pl.num_programs(0)

    def row_copy(src, dst, slot, seq_b, r):
        pos = idx_ref[seq_b * topk + r]
        page = pt_ref[seq_b * n_pages + lax.shift_right_logical(pos, PAGE_SIZE.bit_length() - 1)]
        tok = pos & (PAGE_SIZE - 1)
        return pltpu.make_async_copy(src.at[page, 0, pl.ds(tok, 1)], dst.at[slot, pl.ds(r, 1)], sem.at[slot])

    def start_all(seq_b, slot):
        def body(r, _):
            row_copy(k_hbm, kbuf, slot, seq_b, r).start()
            row_copy(v_hbm, vbuf, slot, seq_b, r).start()
            return 0
        lax.fori_loop(0, topk, body, 0)

    def wait_all(seq_b, slot):
        def body(r, _):
            row_copy(k_hbm, kbuf, slot, seq_b, r).wait()
            row_copy(v_hbm, vbuf, slot, seq_b, r).wait()
            return 0
        lax.fori_loop(0, topk, body, 0)

    slot = b % 2

    @pl.when(b == 0)
    def _():
        start_all(0, 0)

    @pl.when(b + 1 < nb)
    def _():
        start_all(b + 1, 1 - slot)

    wait_all(b, slot)

    qbd = _block_diag(q_ref[0], H_SP)
    scale = HEAD_DIM ** -0.5
    ksel = _page_rows(kbuf, H_SP, (slot,)).astype(BF16)
    vsel = _page_rows(vbuf, H_SP, (slot,)).astype(BF16)
    s = _nt_dot(qbd.astype(BF16), ksel) * scale
    lane = lax.broadcasted_iota(jnp.int32, s.shape, 1)
    s = jnp.where(lane < npast_ref[b], s, -jnp.inf)
    s_new = jnp.sum(qbd * knew_ref[0], axis=1, keepdims=True) * scale
    s_new = jnp.where(selnew_ref[b] > 0, s_new, -jnp.inf)
    m = jnp.maximum(jnp.max(s, axis=1, keepdims=True), s_new)
    e = jnp.exp(s - m)
    e_new = jnp.exp(s_new - m)
    denom = jnp.sum(e, axis=1, keepdims=True) + e_new
    acc = _dot(e.astype(BF16), vsel) + e_new * vnew_ref[0]
    o_ref[0] = _diag_rows(acc / denom, H_SP)


def _sp_sample(page_table, idx, n_past, sel_new, q, k_new, v_new, cache_k, cache_v, topk):
    nb = q.shape[0]
    n_pages = page_table.shape[0] // nb
    vec = pl.BlockSpec((1, 1, W_SP), lambda b, *_: (b, 0, 0))
    grid_spec = pltpu.PrefetchScalarGridSpec(
        num_scalar_prefetch=4,
        grid=(nb,),
        in_specs=[vec, vec, vec, pl.BlockSpec(memory_space=pl.ANY), pl.BlockSpec(memory_space=pl.ANY)],
        out_specs=vec,
        scratch_shapes=[pltpu.VMEM((2, topk, H_SP, HEAD_DIM), F32), pltpu.VMEM((2, topk, H_SP, HEAD_DIM), F32),
                        pltpu.SemaphoreType.DMA((2,))],
    )
    return pl.pallas_call(
        functools.partial(_sp_sample_kernel, topk, n_pages),
        grid_spec=grid_spec,
        out_shape=jax.ShapeDtypeStruct((nb, 1, W_SP), F32),
        compiler_params=_cparams(("arbitrary",)),
        name="sp_sample",
    )(page_table, idx, n_past, sel_new, q, k_new, v_new, cache_k, cache_v)


def _merge_kernel(osb_ref, osp_ref, gsb_ref, gsp_ref, h_ref, wsb_ref, wsp_ref, wout_ref, o_ref, y_s):
    j = pl.program_id(1)

    @pl.when(j == 0)
    def _():
        y_sb = _dot(osb_ref[...].astype(BF16), wsb_ref[...])
        y_sp = _dot(osp_ref[...].astype(BF16), wsp_ref[...])
        y_s[...] = (gsb_ref[...] * y_sb + gsp_ref[...] * y_sp).astype(BF16)

    o_ref[...] = h_ref[...] + _dot(y_s[...], wout_ref[...])


def _merge(o_sb, o_sp, g_sb, g_sp, h, w_sb, w_sp, w_out, n_rows, tm, g_cols):
    tn = 512
    return pl.pallas_call(
        _merge_kernel,
        grid=(n_rows // tm, D_MODEL // tn),
        in_specs=[
            pl.BlockSpec((tm, W_SB), lambda i, j: (i, 0)),
            pl.BlockSpec((tm, W_SP), lambda i, j: (i, 0)),
            pl.BlockSpec((tm, D_MODEL), lambda i, j: (i, g_cols[0])),
            pl.BlockSpec((tm, D_MODEL), lambda i, j: (i, g_cols[1])),
            pl.BlockSpec((tm, tn), lambda i, j: (i, j)),
            pl.BlockSpec((W_SB, D_MODEL), lambda i, j: (0, 0)),
            pl.BlockSpec((W_SP, D_MODEL), lambda i, j: (0, 0)),
            pl.BlockSpec((D_MODEL, tn), lambda i, j: (0, j)),
        ],
        out_specs=pl.BlockSpec((tm, tn), lambda i, j: (i, j)),
        out_shape=jax.ShapeDtypeStruct((n_rows, D_MODEL), F32),
        scratch_shapes=[pltpu.VMEM((tm, D_MODEL), BF16)],
        compiler_params=_cparams(("parallel", "arbitrary")),
        name="merge",
    )(o_sb, o_sp, g_sb, g_sp, h, w_sb, w_sp, w_out)


def kernel(x_prompt, x_sample, cache_k_sb, cache_v_sb, cache_k_sp, cache_v_sp, cache_k_idx, page_table, meta_tokens, g_ffn1, w_ffn1_gate, w_ffn1_up, w_ffn1_down, g_mix, w_in, w_proj_sb, w_proj_sp, w_out, g_ffn2, w_ffn2_gate, w_ffn2_up, w_ffn2_down, g_final):
    batch, seq, _ = x_prompt.shape
    nb, dec_seq, _ = x_sample.shape
    n_pool = cache_k_sb.shape[0]
    n_pages = page_table.shape[1]
    past_len = n_pages * PAGE_SIZE
    assert batch == 1 and dec_seq == 1 and g_ffn1.shape[0] == 1
    assert seq % ROW_TILE == 0 and N_META + nb <= ROW_TILE
    n_rows = seq + ROW_TILE
    samp0 = seq + N_META

    h0 = jnp.concatenate([x_prompt[0], meta_tokens.astype(F32), x_sample[:, 0, :],
                          jnp.zeros((n_rows - samp0 - nb, D_MODEL), F32)], axis=0)
    pos = jnp.concatenate([N_META + jnp.arange(seq, dtype=jnp.int32), jnp.arange(N_META, dtype=jnp.int32),
                           jnp.full((nb,), past_len, jnp.int32), jnp.zeros((n_rows - samp0 - nb,), jnp.int32)])
    tables = _rope_tables(pos)
    wi = w_in[0]
    o_k_idx = 7168
    o_g_sb = o_k_idx + D_IDX + H_IDX
    w_proj = jnp.concatenate([
        wi[:, o_g_sb:],
        wi[:, 0:3 * W_SB],
        wi[:, 3 * W_SB + 2 * W_SP:3 * W_SB + 3 * W_SP],
        wi[:, 3 * W_SB:3 * W_SB + 2 * W_SP],
        wi[:, 3 * W_SB + 3 * W_SP:o_k_idx],
        wi[:, o_k_idx:o_g_sb], jnp.zeros((D_MODEL, PROJ_TILE - D_IDX - H_IDX), F32),
    ], axis=1).astype(BF16)
    bf = lambda w: w[0].astype(BF16)
    row = lambda g: g.reshape(1, D_MODEL)

    h1 = _ffn(h0, n_rows, ROW_TILE, row(g_ffn1[0]), bf(w_ffn1_gate), bf(w_ffn1_up), bf(w_ffn1_down))
    p = _project(h1, row(g_mix[0]), w_proj, tables, n_rows)

    o_sb_p = _sb_prompt(p, seq, n_rows)
    bias = _idx_prompt(p, seq, n_rows)
    o_sp_p = _sp_prompt(p, bias, seq, n_rows)

    ps = p[samp0:samp0 + nb]
    seg = lambda base, w: ps[:, base:base + w]
    o_sb_s = _sb_sample(page_table, seg(P_Q_SB, W_SB).reshape(nb, 1, W_SB), cache_k_sb, cache_v_sb)
    idx, info = _idx_sample(page_table, seg(P_Q_IDX, H_IDX * D_IDX).reshape(nb, H_IDX, D_IDX),
                            seg(P_SMALL + W_IDX_LANE, H_IDX).reshape(nb, H_IDX, 1),
                            seg(P_SMALL, D_IDX).reshape(nb, 1, D_IDX),
                            jnp.swapaxes(cache_k_idx.reshape(n_pool, PAGE_SIZE, D_IDX), 1, 2))
    topk = idx.shape[1]
    o_sp_s = _sp_sample(page_table.reshape(-1), idx[:, :, 0].reshape(-1), info[:, 0, 0], info[:, 1, 0],
                        seg(P_Q_SP, W_SP).reshape(nb, 1, W_SP), seg(P_K_SP, W_SP).reshape(nb, 1, W_SP),
                        seg(P_V_SP, W_SP).reshape(nb, 1, W_SP), cache_k_sp, cache_v_sp, topk)

    wsb, wsp, wo = bf(w_proj_sb), bf(w_proj_sp), bf(w_out)
    ffn2 = (row(g_ffn2[0]), bf(w_ffn2_gate), bf(w_ffn2_up), bf(w_ffn2_down), row(g_final))
    h2_p = _merge(o_sb_p, o_sp_p, p, p, h1, wsb, wsp, wo, seq, ROW_TILE, (P_G_SB // D_MODEL, P_G_SP // D_MODEL))
    y_p = _ffn(h2_p, seq, ROW_TILE, *ffn2)
    h2_s = _merge(o_sb_s.reshape(nb, W_SB), o_sp_s.reshape(nb, W_SP), seg(P_G_SB, D_MODEL), seg(P_G_SP, D_MODEL),
                  h1[samp0:samp0 + nb], wsb, wsp, wo, nb, nb, (0, 0))
    y_s = _ffn(h2_s, nb, nb, *ffn2)

    def prompt_rows(base, w):
        return jnp.concatenate([p[seq:seq + N_META, base:base + w], p[:seq, base:base + w]], axis=0)

    def kv_p(base, nh):
        return prompt_rows(base, nh * HEAD_DIM).reshape(1, 1, seq + N_META, nh, HEAD_DIM)

    def kv_s(base, nh):
        return seg(base, nh * HEAD_DIM).reshape(nb, 1, 1, nh, HEAD_DIM)

    return (y_p.reshape(1, seq, D_MODEL), y_s.reshape(nb, 1, D_MODEL),
            kv_p(P_K_SB, H_SB), kv_p(P_V_SB, H_SB), kv_p(P_K_SP, H_SP), kv_p(P_V_SP, H_SP),
            prompt_rows(P_SMALL, D_IDX).reshape(1, 1, seq + N_META, D_IDX),
            kv_s(P_K_SB, H_SB), kv_s(P_V_SB, H_SB), kv_s(P_K_SP, H_SP), kv_s(P_V_SP, H_SP),
            seg(P_SMALL, D_IDX).reshape(nb, 1, 1, D_IDX))
```

```python
import functools

import jax
import jax.numpy as jnp
from jax import lax
from jax.experimental import pallas as pl
from jax.experimental.pallas import tpu as pltpu

D_MODEL = 2048
N_META = 16
HEAD_DIM = 128
H_SB = 8
H_SP = 8
W_SB = H_SB * HEAD_DIM
W_SP = H_SP * HEAD_DIM
H_IDX = 16
D_IDX = 64
INDEX_TOPK = 256
D_FF = 5632
ROPE_THETA = 500000.0
EPS = 1e-6
PAGE_SIZE = 128

LANES = 128
VMEM_LIMIT_BYTES = 56 * 1024 * 1024

ROW_TILE = 512
FF_TILE = 512
PROJ_TILE = 1024
QBLK = 256
KEY_TILE = 512
TRI = 256
SB_QBLK = 512
HEAD_GROUP = 2
COUNT_ROWS = 128
META_TILE = 128
SB_PAGES_PER_STEP = 8
IDX_PAGES_PER_STEP = 16

P_G_SB = 0
P_G_SP = P_G_SB + D_MODEL
P_Q_SB = P_G_SP + D_MODEL
P_K_SB = P_Q_SB + W_SB
P_V_SB = P_K_SB + W_SB
P_V_SP = P_V_SB + W_SB
P_Q_SP = P_V_SP + W_SP
P_K_SP = P_Q_SP + W_SP
P_Q_IDX = P_K_SP + W_SP
P_SMALL = P_Q_IDX + H_IDX * D_IDX
P_COLS = P_SMALL + PROJ_TILE
W_IDX_LANE = D_IDX

_J_PLAIN = P_Q_SB // PROJ_TILE
_J_ROPE128 = P_Q_SP // PROJ_TILE
_J_ROPE64 = P_Q_IDX // PROJ_TILE
_J_SMALL = P_SMALL // PROJ_TILE

INT_MIN = -(2 ** 31)
F32 = jnp.float32
BF16 = jnp.bfloat16


def _cparams(sem):
    return pltpu.CompilerParams(dimension_semantics=sem, vmem_limit_bytes=VMEM_LIMIT_BYTES)


def _rms(x):
    return x * lax.rsqrt(jnp.mean(x * x, axis=-1, keepdims=True) + EPS)


def _nt_dot(a, b):
    return lax.dot_general(a, b, (((1,), (1,)), ((), ())), preferred_element_type=F32)


def _dot(a, b):
    return jnp.dot(a, b, preferred_element_type=F32)


def _ffn_kernel(final, n_ff, h_ref, g_ref, wg_ref, wu_ref, wd_ref, *rest):
    if final:
        gf_ref, o_ref, u_s, acc_s = rest
    else:
        o_ref, u_s, acc_s = rest
    j = pl.program_id(1)

    @pl.when(j == 0)
    def _():
        u_s[...] = (_rms(h_ref[...]) * g_ref[...]).astype(BF16)
        acc_s[...] = jnp.zeros_like(acc_s)

    u = u_s[...]
    a = _dot(u, wg_ref[...])
    b = _dot(u, wu_ref[...])
    act = (a * jax.nn.sigmoid(a)) * b
    acc_s[...] += _dot(act.astype(BF16), wd_ref[...])

    @pl.when(j == n_ff - 1)
    def _():
        r = h_ref[...] + 0.5 * acc_s[...]
        if final:
            r = _rms(r) * gf_ref[...]
        o_ref[...] = r


def _ffn(h, n_rows, tm, gain, wg, wu, wd, g_final=None):
    final = g_final is not None
    n_ff = D_FF // FF_TILE
    in_specs = [
        pl.BlockSpec((tm, D_MODEL), lambda i, j: (i, 0)),
        pl.BlockSpec((1, D_MODEL), lambda i, j: (0, 0)),
        pl.BlockSpec((D_MODEL, FF_TILE), lambda i, j: (0, j)),
        pl.BlockSpec((D_MODEL, FF_TILE), lambda i, j: (0, j)),
        pl.BlockSpec((FF_TILE, D_MODEL), lambda i, j: (j, 0)),
    ]
    args = [h, gain, wg, wu, wd]
    if final:
        in_specs.append(pl.BlockSpec((1, D_MODEL), lambda i, j: (0, 0)))
        args.append(g_final)
    return pl.pallas_call(
        functools.partial(_ffn_kernel, final, n_ff),
        grid=(n_rows // tm, n_ff),
        in_specs=in_specs,
        out_specs=pl.BlockSpec((tm, D_MODEL), lambda i, j: (i, 0)),
        out_shape=jax.ShapeDtypeStruct((n_rows, D_MODEL), F32),
        scratch_shapes=[pltpu.VMEM((tm, D_MODEL), BF16), pltpu.VMEM((tm, D_MODEL), F32)],
        compiler_params=_cparams(("parallel", "arbitrary")),
        name="ffn_final" if final else "ffn",
    )(*args)


def _rotary(x, c, s1, s2, shift):
    return x * c + pltpu.roll(x, LANES - shift, 1) * s1 + pltpu.roll(x, shift, 1) * s2


def _proj_kernel(h_ref, g_ref, w_ref, t_ref, o_ref, u_s):
    j = pl.program_id(1)

    @pl.when(j == 0)
    def _():
        u_s[...] = (_rms(h_ref[...]) * g_ref[...]).astype(BF16)

    def tables(k):
        return [t_ref[:, (3 * k + m) * LANES:(3 * k + m + 1) * LANES] for m in range(3)]

    def rotary_tiles(y, k, shift, n_tiles):
        c, s1, s2 = tables(k)
        for t in range(PROJ_TILE // LANES):
            x = y[:, t * LANES:(t + 1) * LANES]
            if t < n_tiles:
                x = _rotary(x, c, s1, s2, shift)
            o_ref[:, t * LANES:(t + 1) * LANES] = x

    is_gate = j < _J_PLAIN
    is_plain = (j >= _J_PLAIN) & (j < _J_ROPE128)
    is_r128 = (j >= _J_ROPE128) & (j < _J_ROPE64)
    is_r64 = (j >= _J_ROPE64) & (j < _J_SMALL)
    is_small = j == _J_SMALL

    @pl.when(is_plain)
    def _():
        o_ref[...] = _dot(u_s[...], w_ref[...])

    @pl.when(is_r128)
    def _():
        rotary_tiles(_dot(u_s[...], w_ref[...]), 0, HEAD_DIM // 8, PROJ_TILE // LANES)

    @pl.when(is_r64)
    def _():
        rotary_tiles(_dot(u_s[...], w_ref[...]), 1, D_IDX // 8, PROJ_TILE // LANES)

    @pl.when(is_small)
    def _():
        rotary_tiles(_dot(u_s[...], w_ref[...]), 2, D_IDX // 8, 1)

    @pl.when(is_gate)
    def _():
        o_ref[...] = jax.nn.sigmoid(_dot(u_s[...], w_ref[...]))


def _project(h, gain, w_proj, tables, n_rows):
    return pl.pallas_call(
        _proj_kernel,
        grid=(n_rows // ROW_TILE, P_COLS // PROJ_TILE),
        in_specs=[
            pl.BlockSpec((ROW_TILE, D_MODEL), lambda i, j: (i, 0)),
            pl.BlockSpec((1, D_MODEL), lambda i, j: (0, 0)),
            pl.BlockSpec((D_MODEL, PROJ_TILE), lambda i, j: (0, j)),
            pl.BlockSpec((ROW_TILE, 9 * LANES), lambda i, j: (i, 0)),
        ],
        out_specs=pl.BlockSpec((ROW_TILE, PROJ_TILE), lambda i, j: (i, j)),
        out_shape=jax.ShapeDtypeStruct((n_rows, P_COLS), F32),
        scratch_shapes=[pltpu.VMEM((ROW_TILE, D_MODEL), BF16)],
        compiler_params=_cparams(("parallel", "arbitrary")),
        name="project",
    )(h, gain, w_proj, tables)


def _rope_tables(pos):
    n = pos.shape[0]
    posf = pos.astype(F32)

    def cs(half):
        inv_freq = ROPE_THETA ** (-jnp.arange(half, dtype=F32) / half)
        ang = posf[:, None] * inv_freq[None, :]
        return jnp.cos(ang), jnp.sin(ang)

    def z(w):
        return jnp.zeros((n, w), F32)

    def o(w):
        return jnp.ones((n, w), F32)

    c16, s16 = cs(HEAD_DIM // 8)
    c8, s8 = cs(D_IDX // 8)
    t128 = [jnp.concatenate([c16, c16, o(96)], 1), jnp.concatenate([-s16, z(112)], 1),
            jnp.concatenate([z(16), s16, z(96)], 1)]
    t64 = [jnp.tile(jnp.concatenate([c8, c8, o(48)], 1), (1, 2)), jnp.tile(jnp.concatenate([-s8, z(56)], 1), (1, 2)),
           jnp.tile(jnp.concatenate([z(8), s8, z(48)], 1), (1, 2))]
    tsm = [jnp.concatenate([c8, c8, o(48), (H_IDX ** -0.5) * o(16), o(48)], 1), jnp.concatenate([-s8, z(120)], 1),
           jnp.concatenate([z(8), s8, z(112)], 1)]
    return jnp.concatenate(t128 + t64 + tsm, axis=1)


def _strict_lower_ones(n):
    r = lax.broadcasted_iota(jnp.int32, (n, n), 0)
    c = lax.broadcasted_iota(jnp.int32, (n, n), 1)
    return jnp.where(r > c, 1.0, 0.0).astype(BF16)


def _sb_logs(z):
    log_beta = jnp.minimum(z, 0.0) - jnp.log(1.0 + jnp.exp(-jnp.abs(z)))
    return log_beta, log_beta - z


def _suffix_sums(x, tri):
    m, n = x.shape
    t = tri.shape[0]
    nsub = n // t
    hi = x.astype(BF16)
    lo = (x - hi.astype(F32)).astype(BF16)
    parts = []
    for c in range(nsub):
        parts += [hi[:, c * t:(c + 1) * t], lo[:, c * t:(c + 1) * t]]
    res = _dot(jnp.concatenate(parts, axis=0), tri)
    outs = []
    tail = None
    for c in reversed(range(nsub)):
        s = res[2 * c * m:(2 * c + 1) * m] + res[(2 * c + 1) * m:(2 * c + 2) * m]
        total = jnp.sum(x[:, c * t:(c + 1) * t], axis=1, keepdims=True)
        if tail is not None:
            s = s + tail
            total = total + tail
        outs.append(s)
        tail = total
    return jnp.concatenate(outs[::-1], axis=1), tail


def _sb_tile(q, kt, vt, tri, rsum, acc, valid):
    log_beta, log_keep = _sb_logs(_nt_dot(q, kt) * (HEAD_DIM ** -0.5))
    if valid is not None:
        log_keep = jnp.where(valid, log_keep, 0.0)
    after, total = _suffix_sums(log_keep, tri)
    a = jnp.exp(log_beta + (after + rsum))
    if valid is not None:
        a = jnp.where(valid, a, 0.0)
    return rsum + total, acc + _dot(a.astype(BF16), vt)


def _sb_prompt_kernel(seq, q_ref, k_ref, v_ref, tri_ref, o_ref):
    i = pl.program_id(1)
    tk = KEY_TILE
    qb = SB_QBLK
    tri = tri_ref[...]
    q_row = i * qb + lax.broadcasted_iota(jnp.int32, (qb, tk), 0)
    lane = lax.broadcasted_iota(jnp.int32, (qb, tk), 1)
    heads = [slice(g * HEAD_DIM, (g + 1) * HEAD_DIM) for g in range(HEAD_GROUP)]
    qs = [q_ref[:, hs].astype(BF16) for hs in heads]

    def tile(start, carry, valid, width=tk, tri=tri):
        out = []
        for g, hs in enumerate(heads):
            kt = k_ref[pl.ds(start, width), hs].astype(BF16)
            vt = v_ref[pl.ds(start, width), hs].astype(BF16)
            out.append(_sb_tile(qs[g], kt, vt, tri, carry[g][0], carry[g][1], valid))
        return tuple(out)

    carry = tuple((jnp.zeros((qb, 1), F32), jnp.zeros((qb, HEAD_DIM), F32)) for _ in heads)

    c_diag = (i * qb) // tk
    d_start = pl.multiple_of(c_diag * tk, tk)
    carry = tile(d_start, carry, (d_start + lane) < q_row)
    carry = lax.fori_loop(0, c_diag, lambda it, cr: tile(pl.multiple_of((c_diag - 1 - it) * tk, tk), cr, None), carry)
    meta_valid = lax.broadcasted_iota(jnp.int32, (qb, META_TILE), 1) < N_META
    carry = tile(seq, carry, meta_valid, META_TILE, tri_ref[:META_TILE, :META_TILE])
    for g, hs in enumerate(heads):
        o_ref[:, hs] = carry[g][1]


def _sb_prompt(p, seq, n_rows):
    gw = HEAD_GROUP * HEAD_DIM
    tri = jnp.tril(jnp.ones((TRI, TRI), F32), -1).astype(BF16)
    return pl.pallas_call(
        functools.partial(_sb_prompt_kernel, seq),
        grid=(H_SB // HEAD_GROUP, seq // SB_QBLK),
        in_specs=[
            pl.BlockSpec((SB_QBLK, gw), lambda h, i: (i, P_Q_SB // gw + h)),
            pl.BlockSpec((n_rows, gw), lambda h, i: (0, P_K_SB // gw + h)),
            pl.BlockSpec((n_rows, gw), lambda h, i: (0, P_V_SB // gw + h)),
            pl.BlockSpec((TRI, TRI), lambda h, i: (0, 0)),
        ],
        out_specs=pl.BlockSpec((SB_QBLK, gw), lambda h, i: (i, h)),
        out_shape=jax.ShapeDtypeStruct((seq, W_SB), F32),
        compiler_params=_cparams(("parallel", "arbitrary")),
        name="sb_prompt",
    )(p, p, p, tri)


def _sortable_key(x):
    b = lax.bitcast_convert_type(x, jnp.int32)
    return jnp.where(b < 0, b ^ jnp.int32(0x7FFFFFFF), b)


def _idx_scores(qidx, wcols, ksmall):
    m = qidx.shape[0]
    n_pairs = H_IDX // 2
    lane = lax.broadcasted_iota(jnp.int32, ksmall.shape, 1)
    k_even = jnp.where(lane < D_IDX, ksmall, 0.0)
    k_odd = pltpu.roll(k_even, D_IDX, 1)
    q_stack = jnp.concatenate([qidx[:, hp * LANES:(hp + 1) * LANES] for hp in range(n_pairs)], axis=0)
    s_par = [_nt_dot(q_stack, kk.astype(BF16)) for kk in (k_even, k_odd)]
    acc = None
    for hp in range(n_pairs):
        for par in range(2):
            term = wcols[2 * hp + par] * jnp.maximum(s_par[par][hp * m:(hp + 1) * m], 0.0)
            acc = term if acc is None else acc + term
    return acc


def _topk_threshold(count_ge, shape, topk):
    def body(b, p):
        cand = p + lax.shift_left(jnp.int32(1), jnp.int32(31) - b)
        return jnp.where(count_ge(cand) >= topk, cand, p)

    return lax.fori_loop(0, 32, body, jnp.full(shape, INT_MIN, jnp.int32))


def _idx_prompt_kernel(seq, topk, qidx_ref, qsmall_ref, ksmall_ref, bias_ref, keys_s):
    i = pl.program_id(0)
    tk = KEY_TILE
    n_own = (i * QBLK + QBLK - 1) // tk + 1
    meta_start = seq
    qidx = qidx_ref[...].astype(BF16)
    qsmall = qsmall_ref[...] * (D_IDX ** -0.5)
    wcols = [qsmall[:, W_IDX_LANE + h:W_IDX_LANE + h + 1] for h in range(H_IDX)]
    q_row = i * QBLK + lax.broadcasted_iota(jnp.int32, (QBLK, tk), 0)
    lane = lax.broadcasted_iota(jnp.int32, (QBLK, tk), 1)
    meta_valid = lax.broadcasted_iota(jnp.int32, (QBLK, META_TILE), 1) < N_META

    def fill(start, valid, width):
        sc = _idx_scores(qidx, wcols, ksmall_ref[pl.ds(start, width), :])
        keys_s[:, pl.ds(start, width)] = jnp.where(valid, _sortable_key(sc), INT_MIN)

    def fill_body(c, _):
        start = pl.multiple_of(c * tk, tk)
        fill(start, (start + lane) <= q_row, tk)
        return 0

    lax.fori_loop(0, n_own, fill_body, 0)
    fill(meta_start, meta_valid, META_TILE)

    def count_ge(cand):
        counts = []
        for r0 in range(0, QBLK, COUNT_ROWS):
            cb = jnp.broadcast_to(cand[r0:r0 + COUNT_ROWS], (COUNT_ROWS, LANES))

            def add_tile(start, cnt, width, r0=r0, cb=cb):
                kt = keys_s[pl.ds(r0, COUNT_ROWS), pl.ds(start, width)]
                for t in range(width // LANES):
                    cnt = cnt + jnp.where(kt[:, t * LANES:(t + 1) * LANES] >= cb, 1.0, 0.0)
                return cnt

            cnt = lax.fori_loop(0, n_own, lambda c, cnt: add_tile(pl.multiple_of(c * tk, tk), cnt, tk),
                                jnp.zeros((COUNT_ROWS, LANES), F32))
            cnt = add_tile(meta_start, cnt, META_TILE)
            counts.append(jnp.sum(cnt, axis=1, keepdims=True))
        return jnp.concatenate(counts, axis=0)

    thr = _topk_threshold(count_ge, (QBLK, 1), float(topk))

    bias_ref[...] = jnp.full(bias_ref.shape, -jnp.inf, BF16)

    def emit(start, width):
        kt = keys_s[:, pl.ds(start, width)]
        sel = (kt >= thr) & (kt != INT_MIN)
        bias_ref[:, pl.ds(start, width)] = jnp.where(sel, 0.0, -jnp.inf).astype(BF16)

    def emit_body(c, _):
        emit(pl.multiple_of(c * tk, tk), tk)
        return 0

    lax.fori_loop(0, n_own, emit_body, 0)
    emit(meta_start, META_TILE)


def _idx_prompt(p, seq, n_rows):
    nq = seq // QBLK
    topk = min(INDEX_TOPK, (seq + N_META) // 4)
    return pl.pallas_call(
        functools.partial(_idx_prompt_kernel, seq, topk),
        grid=(nq,),
        in_specs=[
            pl.BlockSpec((QBLK, H_IDX * D_IDX), lambda i: (i, P_Q_IDX // (H_IDX * D_IDX))),
            pl.BlockSpec((QBLK, LANES), lambda i: (i, P_SMALL // LANES)),
            pl.BlockSpec((n_rows, LANES), lambda i: (0, P_SMALL // LANES)),
        ],
        out_specs=pl.BlockSpec((QBLK, n_rows), lambda i: (i, 0)),
        out_shape=jax.ShapeDtypeStruct((seq, n_rows), BF16),
        scratch_shapes=[pltpu.VMEM((QBLK, n_rows), jnp.int32)],
        compiler_params=_cparams(("parallel",)),
        name="idx_prompt",
    )(p, p, p)


def _sp_prompt_kernel(seq, q_ref, k_ref, v_ref, bias_ref, o_ref):
    i = pl.program_id(1)
    tk = KEY_TILE
    n_own = (i * QBLK + QBLK - 1) // tk + 1
    heads = [slice(g * HEAD_DIM, (g + 1) * HEAD_DIM) for g in range(HEAD_GROUP)]
    qs = [q_ref[:, hs].astype(BF16) for hs in heads]

    def tile(start, carry, width=tk):
        bias = bias_ref[:, pl.ds(start, width)].astype(F32)
        out = []
        for g, hs in enumerate(heads):
            m, l, acc = carry[g]
            kt = k_ref[pl.ds(start, width), hs].astype(BF16)
            vt = v_ref[pl.ds(start, width), hs].astype(BF16)
            s = _nt_dot(qs[g], kt) * (HEAD_DIM ** -0.5) + bias
            m_new = jnp.maximum(m, jnp.max(s, axis=1, keepdims=True))
            alpha = jnp.exp(m - m_new)
            e = jnp.exp(s - m_new)
            l = alpha * l + jnp.sum(e, axis=1, keepdims=True)
            acc = alpha * acc + _dot(e.astype(BF16), vt)
            out.append((m_new, l, acc))
        return tuple(out)

    carry = tuple((jnp.full((QBLK, 1), -1e30, F32), jnp.zeros((QBLK, 1), F32), jnp.zeros((QBLK, HEAD_DIM), F32))
                  for _ in heads)
    carry = tile(seq, carry, META_TILE)
    carry = lax.fori_loop(0, n_own, lambda c, cr: tile(pl.multiple_of(c * tk, tk), cr), carry)
    for g, hs in enumerate(heads):
        o_ref[:, hs] = carry[g][2] / carry[g][1]


def _sp_prompt(p, bias, seq, n_rows):
    gw = HEAD_GROUP * HEAD_DIM
    return pl.pallas_call(
        functools.partial(_sp_prompt_kernel, seq),
        grid=(H_SP // HEAD_GROUP, seq // QBLK),
        in_specs=[
            pl.BlockSpec((QBLK, gw), lambda h, i: (i, P_Q_SP // gw + h)),
            pl.BlockSpec((n_rows, gw), lambda h, i: (0, P_K_SP // gw + h)),
            pl.BlockSpec((n_rows, gw), lambda h, i: (0, P_V_SP // gw + h)),
            pl.BlockSpec((QBLK, n_rows), lambda h, i: (i, 0)),
        ],
        out_specs=pl.BlockSpec((QBLK, gw), lambda h, i: (i, h)),
        out_shape=jax.ShapeDtypeStruct((seq, W_SP), F32),
        compiler_params=_cparams(("parallel", "arbitrary")),
        name="sp_prompt",
    )(p, p, p, bias)


Q_ROWS = 16


def _block_diag(qrow, n_heads):
    w = n_heads * HEAD_DIM
    r = lax.broadcasted_iota(jnp.int32, (Q_ROWS, w), 0)
    c = lax.broadcasted_iota(jnp.int32, (Q_ROWS, w), 1)
    return jnp.where((c // HEAD_DIM) == r, jnp.broadcast_to(qrow, (Q_ROWS, w)), 0.0)


def _diag_rows(x, n_heads):
    w = n_heads * HEAD_DIM
    r = lax.broadcasted_iota(jnp.int32, (Q_ROWS, w), 0)
    c = lax.broadcasted_iota(jnp.int32, (Q_ROWS, w), 1)
    return jnp.sum(jnp.where((c // HEAD_DIM) == r, x, 0.0), axis=0, keepdims=True)


def _page_rows(ref, n_heads, lead=(0, 0)):
    r = ref.at[lead]
    t = r.shape[0]
    flat = r.reshape(t * n_heads, HEAD_DIM)
    return jnp.concatenate([flat[pl.ds(h, t, stride=n_heads), :] for h in range(n_heads)], axis=1)


def _sb_sample_kernel(n_steps, pt_ref, q_ref, *rest):
    npg = SB_PAGES_PER_STEP
    k_refs = rest[:npg]
    v_refs = rest[npg:2 * npg]
    o_ref, rsum_s, acc_s = rest[2 * npg:]
    j = pl.program_id(1)

    @pl.when(j == 0)
    def _():
        rsum_s[...] = jnp.zeros_like(rsum_s)
        acc_s[...] = jnp.zeros_like(acc_s)

    q = _block_diag(q_ref[0], H_SB).astype(BF16)
    tri = _strict_lower_ones(PAGE_SIZE)
    z = jnp.concatenate([_nt_dot(q, _page_rows(k_refs[k], H_SB).astype(BF16)) for k in range(npg)], axis=0)
    log_beta, log_keep = _sb_logs(z * (HEAD_DIM ** -0.5))
    after, _ = _suffix_sums(log_keep, tri)
    totals = jnp.sum(log_keep, axis=1, keepdims=True)
    rsum = rsum_s[...]
    carried = []
    for k in range(npg):
        carried.append(rsum)
        rsum = rsum + totals[k * Q_ROWS:(k + 1) * Q_ROWS]
    rsum_s[...] = rsum
    a = jnp.exp(log_beta + (after + jnp.concatenate(carried, axis=0))).astype(BF16)
    acc = acc_s[...]
    for k in range(npg):
        acc = acc + _dot(a[k * Q_ROWS:(k + 1) * Q_ROWS], _page_rows(v_refs[k], H_SB).astype(BF16))
    acc_s[...] = acc

    @pl.when(j == n_steps - 1)
    def _():
        o_ref[0] = _diag_rows(acc, H_SB)


def _sb_sample(page_table, q, cache_k, cache_v):
    nb, n_pages = page_table.shape
    npg = SB_PAGES_PER_STEP
    n_steps = n_pages // npg

    def page_spec(k):
        return pl.BlockSpec((1, 1, PAGE_SIZE, H_SB, HEAD_DIM),
                            lambda b, j, pt: (pt[b, n_pages - 1 - (j * npg + k)], 0, 0, 0, 0))

    grid_spec = pltpu.PrefetchScalarGridSpec(
        num_scalar_prefetch=1,
        grid=(nb, n_steps),
        in_specs=[pl.BlockSpec((1, 1, W_SB), lambda b, j, pt: (b, 0, 0))]
        + [page_spec(k) for k in range(npg)] + [page_spec(k) for k in range(npg)],
        out_specs=pl.BlockSpec((1, 1, W_SB), lambda b, j, pt: (b, 0, 0)),
        scratch_shapes=[pltpu.VMEM((Q_ROWS, 1), F32), pltpu.VMEM((Q_ROWS, W_SB), F32)],
    )
    return pl.pallas_call(
        functools.partial(_sb_sample_kernel, n_steps),
        grid_spec=grid_spec,
        out_shape=jax.ShapeDtypeStruct((nb, 1, W_SB), F32),
        compiler_params=_cparams(("parallel", "arbitrary")),
        name="sb_sample",
    )(page_table, q, *([cache_k] * npg), *([cache_v] * npg))


def _idx_scores_sample_kernel(pt_ref, q_ref, w_ref, knew_ref, *rest):
    del pt_ref
    npg = IDX_PAGES_PER_STEP
    k_refs = rest[:npg]
    sc_ref, scnew_ref = rest[npg:]
    j = pl.program_id(1)
    q = q_ref[0].astype(BF16)
    w = w_ref[0]
    for k in range(npg):
        s = _dot(q, k_refs[k][0].astype(BF16)) * (D_IDX ** -0.5)
        sc_ref[0, pl.ds(j * npg + k, 1), :] = jnp.sum(w * jnp.maximum(s, 0.0), axis=0, keepdims=True)

    @pl.when(j == 0)
    def _():
        s_new = jnp.sum(q.astype(F32) * knew_ref[0].astype(BF16).astype(F32), axis=1, keepdims=True)
        sc_new = jnp.sum(w * jnp.maximum(s_new * (D_IDX ** -0.5), 0.0), axis=0, keepdims=True)
        scnew_ref[0] = jnp.broadcast_to(sc_new, (8, LANES))


def _idx_select_sample_kernel(nb, n_pages, topk, sc_ref, scnew_ref, idx_ref, info_ref, gt_s, eq_s, st_s):
    keys = _sortable_key(sc_ref[...])
    key_new = _sortable_key(scnew_ref[...])[:, :1, :1]

    def total(x):
        return jnp.sum(jnp.sum(x, axis=1, keepdims=True), axis=2, keepdims=True)

    def count_ge(cand):
        return total(jnp.where(keys >= cand, 1.0, 0.0)) + jnp.where(key_new >= cand, 1.0, 0.0)

    thr = _topk_threshold(count_ge, (nb, 1, 1), float(topk))
    gt = jnp.where(keys > thr, 1.0, 0.0)
    eq = jnp.where(keys == thr, 1.0, 0.0)
    gt_s[...] = gt
    eq_s[...] = eq
    n_gt = total(gt) + jnp.where(key_new > thr, 1.0, 0.0)
    need_all = float(topk) - n_gt
    n_eq_all = total(eq)
    new_gt = jnp.where(key_new > thr, 1.0, 0.0)
    new_eq = jnp.where(key_new == thr, 1.0, 0.0)
    lane3 = lax.broadcasted_iota(jnp.int32, (nb, 8, LANES), 2)
    st_s[...] = jnp.where(lane3 == 0, need_all,
                          jnp.where(lane3 == 1, n_eq_all, jnp.where(lane3 == 2, new_gt, new_eq)))

    r = lax.broadcasted_iota(jnp.int32, (n_pages, n_pages), 0)
    c = lax.broadcasted_iota(jnp.int32, (n_pages, n_pages), 1)
    pages_before = jnp.where(c < r, 1.0, 0.0).astype(BF16)
    pages_before_t = jnp.where(r < c, 1.0, 0.0).astype(BF16)
    rl = lax.broadcasted_iota(jnp.int32, (PAGE_SIZE, PAGE_SIZE), 0)
    cl = lax.broadcasted_iota(jnp.int32, (PAGE_SIZE, PAGE_SIZE), 1)
    lanes_before = jnp.where(rl < cl, 1.0, 0.0).astype(BF16)
    ones_q = jnp.ones((Q_ROWS, PAGE_SIZE), BF16)
    slot = lax.broadcasted_iota(jnp.int32, (topk, n_pages), 0).astype(F32)
    page_id = lax.broadcasted_iota(jnp.int32, (topk, n_pages), 1).astype(F32)
    tok_id = lax.broadcasted_iota(jnp.int32, (topk, PAGE_SIZE), 1).astype(F32)
    row8 = lax.broadcasted_iota(jnp.int32, (8, LANES), 0)

    def per_sequence(b, _):
        gt_b = gt_s[b]
        eq_b = eq_s[b]
        st = st_s[b]
        need = st[:1, 0:1]
        n_eq = st[:1, 1:2]

        def lane_rank(x):
            return _dot(x.astype(BF16), lanes_before)

        def page_offsets(x):
            cnt = jnp.broadcast_to(jnp.sum(x, axis=1, keepdims=True), x.shape)
            return _dot(pages_before, cnt.astype(BF16))

        eq_rank = page_offsets(eq_b) + lane_rank(eq_b)
        sel = jnp.maximum(gt_b, jnp.where(eq_rank < need, eq_b, 0.0))
        sel_bf = sel.astype(BF16)
        n_past = jnp.sum(jnp.sum(sel, axis=1, keepdims=True), axis=0, keepdims=True)
        sel_new = jnp.maximum(st[:1, 2:3], jnp.where(n_eq < need, st[:1, 3:4], 0.0))

        cnt_l = _nt_dot(ones_q, sel_bf)
        off_l = _dot(cnt_l.astype(BF16), pages_before_t)
        off = off_l[:1]
        end = off + cnt_l[:1]
        in_page = jnp.where((slot >= off) & (slot < end), 1.0, 0.0)
        page_of = jnp.sum(in_page * page_id, axis=1, keepdims=True)
        local = slot[:, :1] - jnp.sum(in_page * off, axis=1, keepdims=True)
        in_page_bf = in_page.astype(BF16)
        sel_rows = _dot(in_page_bf, sel_bf)
        rank_rows = _dot(in_page_bf, lane_rank(sel).astype(BF16))
        hit = jnp.where(rank_rows == local, sel_rows, 0.0)
        tok_of = jnp.sum(hit * tok_id, axis=1, keepdims=True)
        pos = page_of * PAGE_SIZE + tok_of
        idx_ref[b] = jnp.broadcast_to(pos, (topk, LANES)).astype(jnp.int32)
        info_ref[b] = jnp.where(row8 == 0, n_past, sel_new).astype(jnp.int32)
        return 0

    lax.fori_loop(0, nb, per_sequence, 0)


def _idx_sample(page_table, q, w, k_new, cache_k_idx):
    nb, n_pages = page_table.shape
    npg = IDX_PAGES_PER_STEP
    n_steps = n_pages // npg
    topk = min(INDEX_TOPK, (n_pages * PAGE_SIZE + 1) // 4)

    def page_spec(k):
        return pl.BlockSpec((1, D_IDX, PAGE_SIZE), lambda b, j, pt: (pt[b, j * npg + k], 0, 0))

    grid_spec = pltpu.PrefetchScalarGridSpec(
        num_scalar_prefetch=1,
        grid=(nb, n_steps),
        in_specs=[
            pl.BlockSpec((1, H_IDX, D_IDX), lambda b, j, pt: (b, 0, 0)),
            pl.BlockSpec((1, H_IDX, 1), lambda b, j, pt: (b, 0, 0)),
            pl.BlockSpec((1, 1, D_IDX), lambda b, j, pt: (b, 0, 0)),
        ] + [page_spec(k) for k in range(npg)],
        out_specs=[
            pl.BlockSpec((1, n_pages, PAGE_SIZE), lambda b, j, pt: (b, 0, 0)),
            pl.BlockSpec((1, 8, LANES), lambda b, j, pt: (b, 0, 0)),
        ],
    )
    scores, sc_new = pl.pallas_call(
        _idx_scores_sample_kernel,
        grid_spec=grid_spec,
        out_shape=[jax.ShapeDtypeStruct((nb, n_pages, PAGE_SIZE), F32),
                   jax.ShapeDtypeStruct((nb, 8, LANES), F32)],
        compiler_params=_cparams(("parallel", "arbitrary")),
        name="idx_sample_scores",
    )(page_table, q, w, k_new, *([cache_k_idx] * npg))
    return pl.pallas_call(
        functools.partial(_idx_select_sample_kernel, nb, n_pages, topk),
        out_shape=[jax.ShapeDtypeStruct((nb, topk, LANES), jnp.int32),
                   jax.ShapeDtypeStruct((nb, 8, LANES), jnp.int32)],
        scratch_shapes=[pltpu.VMEM((nb, n_pages, PAGE_SIZE), F32), pltpu.VMEM((nb, n_pages, PAGE_SIZE), F32),
                        pltpu.VMEM((nb, 8, LANES), F32)],
        compiler_params=pltpu.CompilerParams(vmem_limit_bytes=VMEM_LIMIT_BYTES),
        name="idx_sample_select",
    )(scores, sc_new)


def _sp_sample_kernel(topk, n_pages, pt_ref, idx_ref, npast_ref, selnew_ref, q_ref, knew_ref, vnew_ref,
                      k_hbm, v_hbm, o_ref, kbuf, vbuf, sem):
    b = pl.program_id(0)
    nb = pl.num_programs(0)

    def row_copy(src, dst, slot, seq_b, r):
        pos = idx_ref[seq_b * topk + r]
        page = pt_ref[seq_b * n_pages + lax.shift_right_logical(pos, PAGE_SIZE.bit_length() - 1)]
        tok = pos & (PAGE_SIZE - 1)
        return pltpu.make_async_copy(src.at[page, 0, pl.ds(tok, 1)], dst.at[slot, pl.ds(r, 1)], sem.at[slot])

    def start_all(seq_b, slot):
        def body(r, _):
            row_copy(k_hbm, kbuf, slot, seq_b, r).start()
            row_copy(v_hbm, vbuf, slot, seq_b, r).start()
            return 0
        lax.fori_loop(0, topk, body, 0)

    def wait_all(seq_b, slot):
        def body(r, _):
            row_copy(k_hbm, kbuf, slot, seq_b, r).wait()
            row_copy(v_hbm, vbuf, slot, seq_b, r).wait()
            return 0
        lax.fori_loop(0, topk, body, 0)

    slot = b % 2

    @pl.when(b == 0)
    def _():
        start_all(0, 0)

    @pl.when(b + 1 < nb)
    def _():
        start_all(b + 1, 1 - slot)

    wait_all(b, slot)

    qbd = _block_diag(q_ref[0], H_SP)
    scale = HEAD_DIM ** -0.5
    ksel = _page_rows(kbuf, H_SP, (slot,)).astype(BF16)
    vsel = _page_rows(vbuf, H_SP, (slot,)).astype(BF16)
    s = _nt_dot(qbd.astype(BF16), ksel) * scale
    lane = lax.broadcasted_iota(jnp.int32, s.shape, 1)
    s = jnp.where(lane < npast_ref[b], s, -jnp.inf)
    s_new = jnp.sum(qbd * knew_ref[0], axis=1, keepdims=True) * scale
    s_new = jnp.where(selnew_ref[b] > 0, s_new, -jnp.inf)
    m = jnp.maximum(jnp.max(s, axis=1, keepdims=True), s_new)
    e = jnp.exp(s - m)
    e_new = jnp.exp(s_new - m)
    denom = jnp.sum(e, axis=1, keepdims=True) + e_new
    acc = _dot(e.astype(BF16), vsel) + e_new * vnew_ref[0]
    o_ref[0] = _diag_rows(acc / denom, H_SP)


def _sp_sample(page_table, idx, n_past, sel_new, q, k_new, v_new, cache_k, cache_v, topk):
    nb = q.shape[0]
    n_pages = page_table.shape[0] // nb
    vec = pl.BlockSpec((1, 1, W_SP), lambda b, *_: (b, 0, 0))
    grid_spec = pltpu.PrefetchScalarGridSpec(
        num_scalar_prefetch=4,
        grid=(nb,),
        in_specs=[vec, vec, vec, pl.BlockSpec(memory_space=pl.ANY), pl.BlockSpec(memory_space=pl.ANY)],
        out_specs=vec,
        scratch_shapes=[pltpu.VMEM((2, topk, H_SP, HEAD_DIM), F32), pltpu.VMEM((2, topk, H_SP, HEAD_DIM), F32),
                        pltpu.SemaphoreType.DMA((2,))],
    )
    return pl.pallas_call(
        functools.partial(_sp_sample_kernel, topk, n_pages),
        grid_spec=grid_spec,
        out_shape=jax.ShapeDtypeStruct((nb, 1, W_SP), F32),
        compiler_params=_cparams(("arbitrary",)),
        name="sp_sample",
    )(page_table, idx, n_past, sel_new, q, k_new, v_new, cache_k, cache_v)


def _merge_kernel(osb_ref, osp_ref, gsb_ref, gsp_ref, h_ref, wsb_ref, wsp_ref, wout_ref, o_ref, y_s):
    j = pl.program_id(1)

    @pl.when(j == 0)
    def _():
        y_sb = _dot(osb_ref[...].astype(BF16), wsb_ref[...])
        y_sp = _dot(osp_ref[...].astype(BF16), wsp_ref[...])
        y_s[...] = (gsb_ref[...] * y_sb + gsp_ref[...] * y_sp).astype(BF16)

    o_ref[...] = h_ref[...] + _dot(y_s[...], wout_ref[...])


def _merge(o_sb, o_sp, g_sb, g_sp, h, w_sb, w_sp, w_out, n_rows, tm, g_cols):
    tn = 512
    return pl.pallas_call(
        _merge_kernel,
        grid=(n_rows // tm, D_MODEL // tn),
        in_specs=[
            pl.BlockSpec((tm, W_SB), lambda i, j: (i, 0)),
            pl.BlockSpec((tm, W_SP), lambda i, j: (i, 0)),
            pl.BlockSpec((tm, D_MODEL), lambda i, j: (i, g_cols[0])),
            pl.BlockSpec((tm, D_MODEL), lambda i, j: (i, g_cols[1])),
            pl.BlockSpec((tm, tn), lambda i, j: (i, j)),
            pl.BlockSpec((W_SB, D_MODEL), lambda i, j: (0, 0)),
            pl.BlockSpec((W_SP, D_MODEL), lambda i, j: (0, 0)),
            pl.BlockSpec((D_MODEL, tn), lambda i, j: (0, j)),
        ],
        out_specs=pl.BlockSpec((tm, tn), lambda i, j: (i, j)),
        out_shape=jax.ShapeDtypeStruct((n_rows, D_MODEL), F32),
        scratch_shapes=[pltpu.VMEM((tm, D_MODEL), BF16)],
        compiler_params=_cparams(("parallel", "arbitrary")),
        name="merge",
    )(o_sb, o_sp, g_sb, g_sp, h, w_sb, w_sp, w_out)


def kernel(x_prompt, x_sample, cache_k_sb, cache_v_sb, cache_k_sp, cache_v_sp, cache_k_idx, page_table, meta_tokens, g_ffn1, w_ffn1_gate, w_ffn1_up, w_ffn1_down, g_mix, w_in, w_proj_sb, w_proj_sp, w_out, g_ffn2, w_ffn2_gate, w_ffn2_up, w_ffn2_down, g_final):
    batch, seq, _ = x_prompt.shape
    nb, dec_seq, _ = x_sample.shape
    n_pool = cache_k_sb.shape[0]
    n_pages = page_table.shape[1]
    past_len = n_pages * PAGE_SIZE
    assert batch == 1 and dec_seq == 1 and g_ffn1.shape[0] == 1
    assert seq % ROW_TILE == 0 and N_META + nb <= ROW_TILE
    n_rows = seq + ROW_TILE
    samp0 = seq + N_META

    h0 = jnp.concatenate([x_prompt[0], meta_tokens.astype(F32), x_sample[:, 0, :],
                          jnp.zeros((n_rows - samp0 - nb, D_MODEL), F32)], axis=0)
    pos = jnp.concatenate([N_META + jnp.arange(seq, dtype=jnp.int32), jnp.arange(N_META, dtype=jnp.int32),
                           jnp.full((nb,), past_len, jnp.int32), jnp.zeros((n_rows - samp0 - nb,), jnp.int32)])
    tables = _rope_tables(pos)
    wi = w_in[0]
    o_k_idx = 7168
    o_g_sb = o_k_idx + D_IDX + H_IDX
    w_proj = jnp.concatenate([
        wi[:, o_g_sb:],
        wi[:, 0:3 * W_SB],
        wi[:, 3 * W_SB + 2 * W_SP:3 * W_SB + 3 * W_SP],
        wi[:, 3 * W_SB:3 * W_SB + 2 * W_SP],
        wi[:, 3 * W_SB + 3 * W_SP:o_k_idx],
        wi[:, o_k_idx:o_g_sb], jnp.zeros((D_MODEL, PROJ_TILE - D_IDX - H_IDX), F32),
    ], axis=1).astype(BF16)
    bf = lambda w: w[0].astype(BF16)
    row = lambda g: g.reshape(1, D_MODEL)

    h1 = _ffn(h0, n_rows, ROW_TILE, row(g_ffn1[0]), bf(w_ffn1_gate), bf(w_ffn1_up), bf(w_ffn1_down))
    p = _project(h1, row(g_mix[0]), w_proj, tables, n_rows)

    o_sb_p = _sb_prompt(p, seq, n_rows)
    bias = _idx_prompt(p, seq, n_rows)
    o_sp_p = _sp_prompt(p, bias, seq, n_rows)

    ps = p[samp0:samp0 + nb]
    seg = lambda base, w: ps[:, base:base + w]
    o_sb_s = _sb_sample(page_table, seg(P_Q_SB, W_SB).reshape(nb, 1, W_SB), cache_k_sb, cache_v_sb)
    idx, info = _idx_sample(page_table, seg(P_Q_IDX, H_IDX * D_IDX).reshape(nb, H_IDX, D_IDX),
                            seg(P_SMALL + W_IDX_LANE, H_IDX).reshape(nb, H_IDX, 1),
                            seg(P_SMALL, D_IDX).reshape(nb, 1, D_IDX),
                            jnp.swapaxes(cache_k_idx.reshape(n_pool, PAGE_SIZE, D_IDX), 1, 2))
    topk = idx.shape[1]
    o_sp_s = _sp_sample(page_table.reshape(-1), idx[:, :, 0].reshape(-1), info[:, 0, 0], info[:, 1, 0],
                        seg(P_Q_SP, W_SP).reshape(nb, 1, W_SP), seg(P_K_SP, W_SP).reshape(nb, 1, W_SP),
                        seg(P_V_SP, W_SP).reshape(nb, 1, W_SP), cache_k_sp, cache_v_sp, topk)

    wsb, wsp, wo = bf(w_proj_sb), bf(w_proj_sp), bf(w_out)
    ffn2 = (row(g_ffn2[0]), bf(w_ffn2_gate), bf(w_ffn2_up), bf(w_ffn2_down), row(g_final))
    h2_p = _merge(o_sb_p, o_sp_p, p, p, h1, wsb, wsp, wo, seq, ROW_TILE, (P_G_SB // D_MODEL, P_G_SP // D_MODEL))
    y_p = _ffn(h2_p, seq, ROW_TILE, *ffn2)
    h2_s = _merge(o_sb_s.reshape(nb, W_SB), o_sp_s.reshape(nb, W_SP), seg(P_G_SB, D_MODEL), seg(P_G_SP, D_MODEL),
                  h1[samp0:samp0 + nb], wsb, wsp, wo, nb, nb, (0, 0))
    y_s = _ffn(h2_s, nb, nb, *ffn2)

    def prompt_rows(base, w):
        return jnp.concatenate([p[seq:seq + N_META, base:base + w], p[:seq, base:base + w]], axis=0)

    def kv_p(base, nh):
        return prompt_rows(base, nh * HEAD_DIM).reshape(1, 1, seq + N_META, nh, HEAD_DIM)

    def kv_s(base, nh):
        return seg(base, nh * HEAD_DIM).reshape(nb, 1, 1, nh, HEAD_DIM)

    return (y_p.reshape(1, seq, D_MODEL), y_s.reshape(nb, 1, D_MODEL),
            kv_p(P_K_SB, H_SB), kv_p(P_V_SB, H_SB), kv_p(P_K_SP, H_SP), kv_p(P_V_SP, H_SP),
            prompt_rows(P_SMALL, D_IDX).reshape(1, 1, seq + N_META, D_IDX),
            kv_s(P_K_SB, H_SB), kv_s(P_V_SB, H_SB), kv_s(P_K_SP, H_SP), kv_s(P_V_SP, H_SP),
            seg(P_SMALL, D_IDX).reshape(nb, 1, 1, D_IDX))
```

```python
import functools

import jax
import jax.numpy as jnp
from jax import lax
from jax.experimental import pallas as pl
from jax.experimental.pallas import tpu as pltpu

D_MODEL = 2048
N_META = 16
HEAD_DIM = 128
H_SB = 8
H_SP = 8
W_SB = H_SB * HEAD_DIM
W_SP = H_SP * HEAD_DIM
H_IDX = 16
D_IDX = 64
INDEX_TOPK = 256
D_FF = 5632
ROPE_THETA = 500000.0
EPS = 1e-6
PAGE_SIZE = 128

LANES = 128
VMEM_LIMIT_BYTES = 56 * 1024 * 1024

ROW_TILE = 512
FF_TILE = 512
PROJ_TILE = 1024
QBLK = 256
KEY_TILE = 512
SB_KEY_TILE = 512
SP_KEY_TILE = 1024
TRI = 256
SB_QBLK = 512
HEAD_GROUP = 2
COUNT_ROWS = 128
META_TILE = 128
SB_PAGES_PER_STEP = 8
IDX_PAGES_PER_STEP = 16

P_G_SB = 0
P_G_SP = P_G_SB + D_MODEL
P_Q_SB = P_G_SP + D_MODEL
P_K_SB = P_Q_SB + W_SB
P_V_SB = P_K_SB + W_SB
P_V_SP = P_V_SB + W_SB
P_Q_SP = P_V_SP + W_SP
P_K_SP = P_Q_SP + W_SP
P_Q_IDX = P_K_SP + W_SP
P_SMALL = P_Q_IDX + H_IDX * D_IDX
P_COLS = P_SMALL + PROJ_TILE
W_IDX_LANE = D_IDX

_J_PLAIN = P_Q_SB // PROJ_TILE
_J_ROPE128 = P_Q_SP // PROJ_TILE
_J_ROPE64 = P_Q_IDX // PROJ_TILE
_J_SMALL = P_SMALL // PROJ_TILE

INT_MIN = -(2 ** 31)
F32 = jnp.float32
BF16 = jnp.bfloat16


def _cparams(sem):
    return pltpu.CompilerParams(dimension_semantics=sem, vmem_limit_bytes=VMEM_LIMIT_BYTES)


def _rms(x):
    return x * lax.rsqrt(jnp.mean(x * x, axis=-1, keepdims=True) + EPS)


def _nt_dot(a, b):
    return lax.dot_general(a, b, (((1,), (1,)), ((), ())), preferred_element_type=F32)


def _dot(a, b):
    return jnp.dot(a, b, preferred_element_type=F32)


def _ffn_kernel(final, n_ff, h_ref, g_ref, wg_ref, wu_ref, wd_ref, *rest):
    if final:
        gf_ref, o_ref, u_s, acc_s = rest
    else:
        o_ref, u_s, acc_s = rest
    j = pl.program_id(1)

    @pl.when(j == 0)
    def _():
        u_s[...] = (_rms(h_ref[...]) * g_ref[...]).astype(BF16)
        acc_s[...] = jnp.zeros_like(acc_s)

    u = u_s[...]
    a = _dot(u, wg_ref[...])
    b = _dot(u, wu_ref[...])
    act = (a * jax.nn.sigmoid(a)) * b
    acc_s[...] += _dot(act.astype(BF16), wd_ref[...])

    @pl.when(j == n_ff - 1)
    def _():
        r = h_ref[...] + 0.5 * acc_s[...]
        if final:
            r = _rms(r) * gf_ref[...]
        o_ref[...] = r


def _ffn(h, n_rows, tm, gain, wg, wu, wd, g_final=None):
    final = g_final is not None
    n_ff = D_FF // FF_TILE
    in_specs = [
        pl.BlockSpec((tm, D_MODEL), lambda i, j: (i, 0)),
        pl.BlockSpec((1, D_MODEL), lambda i, j: (0, 0)),
        pl.BlockSpec((D_MODEL, FF_TILE), lambda i, j: (0, j)),
        pl.BlockSpec((D_MODEL, FF_TILE), lambda i, j: (0, j)),
        pl.BlockSpec((FF_TILE, D_MODEL), lambda i, j: (j, 0)),
    ]
    args = [h, gain, wg, wu, wd]
    if final:
        in_specs.append(pl.BlockSpec((1, D_MODEL), lambda i, j: (0, 0)))
        args.append(g_final)
    return pl.pallas_call(
        functools.partial(_ffn_kernel, final, n_ff),
        grid=(n_rows // tm, n_ff),
        in_specs=in_specs,
        out_specs=pl.BlockSpec((tm, D_MODEL), lambda i, j: (i, 0)),
        out_shape=jax.ShapeDtypeStruct((n_rows, D_MODEL), F32),
        scratch_shapes=[pltpu.VMEM((tm, D_MODEL), BF16), pltpu.VMEM((tm, D_MODEL), F32)],
        compiler_params=_cparams(("parallel", "arbitrary")),
        name="ffn_final" if final else "ffn",
    )(*args)


def _rotary(x, c, s1, s2, shift):
    return x * c + pltpu.roll(x, LANES - shift, 1) * s1 + pltpu.roll(x, shift, 1) * s2


def _proj_kernel(h_ref, g_ref, w_ref, t_ref, o_ref, u_s):
    j = pl.program_id(1)

    @pl.when(j == 0)
    def _():
        u_s[...] = (_rms(h_ref[...]) * g_ref[...]).astype(BF16)

    def tables(k):
        return [t_ref[:, (3 * k + m) * LANES:(3 * k + m + 1) * LANES] for m in range(3)]

    def rotary_tiles(y, k, shift, n_tiles):
        c, s1, s2 = tables(k)
        for t in range(PROJ_TILE // LANES):
            x = y[:, t * LANES:(t + 1) * LANES]
            if t < n_tiles:
                x = _rotary(x, c, s1, s2, shift)
            o_ref[:, t * LANES:(t + 1) * LANES] = x

    is_gate = j < _J_PLAIN
    is_plain = (j >= _J_PLAIN) & (j < _J_ROPE128)
    is_r128 = (j >= _J_ROPE128) & (j < _J_ROPE64)
    is_r64 = (j >= _J_ROPE64) & (j < _J_SMALL)
    is_small = j == _J_SMALL

    @pl.when(is_plain)
    def _():
        o_ref[...] = _dot(u_s[...], w_ref[...])

    @pl.when(is_r128)
    def _():
        rotary_tiles(_dot(u_s[...], w_ref[...]), 0, HEAD_DIM // 8, PROJ_TILE // LANES)

    @pl.when(is_r64)
    def _():
        rotary_tiles(_dot(u_s[...], w_ref[...]), 1, D_IDX // 8, PROJ_TILE // LANES)

    @pl.when(is_small)
    def _():
        rotary_tiles(_dot(u_s[...], w_ref[...]), 2, D_IDX // 8, 1)

    @pl.when(is_gate)
    def _():
        o_ref[...] = jax.nn.sigmoid(_dot(u_s[...], w_ref[...]))


def _project(h, gain, w_proj, tables, n_rows):
    return pl.pallas_call(
        _proj_kernel,
        grid=(n_rows // ROW_TILE, P_COLS // PROJ_TILE),
        in_specs=[
            pl.BlockSpec((ROW_TILE, D_MODEL), lambda i, j: (i, 0)),
            pl.BlockSpec((1, D_MODEL), lambda i, j: (0, 0)),
            pl.BlockSpec((D_MODEL, PROJ_TILE), lambda i, j: (0, j)),
            pl.BlockSpec((ROW_TILE, 9 * LANES), lambda i, j: (i, 0)),
        ],
        out_specs=pl.BlockSpec((ROW_TILE, PROJ_TILE), lambda i, j: (i, j)),
        out_shape=jax.ShapeDtypeStruct((n_rows, P_COLS), F32),
        scratch_shapes=[pltpu.VMEM((ROW_TILE, D_MODEL), BF16)],
        compiler_params=_cparams(("parallel", "arbitrary")),
        name="project",
    )(h, gain, w_proj, tables)


def _rope_tables(pos):
    n = pos.shape[0]
    posf = pos.astype(F32)

    def cs(half):
        inv_freq = ROPE_THETA ** (-jnp.arange(half, dtype=F32) / half)
        ang = posf[:, None] * inv_freq[None, :]
        return jnp.cos(ang), jnp.sin(ang)

    def z(w):
        return jnp.zeros((n, w), F32)

    def o(w):
        return jnp.ones((n, w), F32)

    c16, s16 = cs(HEAD_DIM // 8)
    c8, s8 = cs(D_IDX // 8)
    t128 = [jnp.concatenate([c16, c16, o(96)], 1), jnp.concatenate([-s16, z(112)], 1),
            jnp.concatenate([z(16), s16, z(96)], 1)]
    t64 = [jnp.tile(jnp.concatenate([c8, c8, o(48)], 1), (1, 2)), jnp.tile(jnp.concatenate([-s8, z(56)], 1), (1, 2)),
           jnp.tile(jnp.concatenate([z(8), s8, z(48)], 1), (1, 2))]
    tsm = [jnp.concatenate([c8, c8, o(48), (H_IDX ** -0.5) * o(16), o(48)], 1), jnp.concatenate([-s8, z(120)], 1),
           jnp.concatenate([z(8), s8, z(112)], 1)]
    return jnp.concatenate(t128 + t64 + tsm, axis=1)


def _strict_lower_ones(n):
    r = lax.broadcasted_iota(jnp.int32, (n, n), 0)
    c = lax.broadcasted_iota(jnp.int32, (n, n), 1)
    return jnp.where(r > c, 1.0, 0.0).astype(BF16)


def _sb_logs(z):
    log_beta = jnp.minimum(z, 0.0) - jnp.log(1.0 + jnp.exp(-jnp.abs(z)))
    return log_beta, log_beta - z


def _suffix_sums(x, tri):
    m, n = x.shape
    t = tri.shape[0]
    nsub = n // t
    hi = x.astype(BF16)
    lo = (x - hi.astype(F32)).astype(BF16)
    parts = []
    for c in range(nsub):
        parts += [hi[:, c * t:(c + 1) * t], lo[:, c * t:(c + 1) * t]]
    res = _dot(jnp.concatenate(parts, axis=0), tri)
    outs = []
    tail = None
    for c in reversed(range(nsub)):
        s = res[2 * c * m:(2 * c + 1) * m] + res[(2 * c + 1) * m:(2 * c + 2) * m]
        total = jnp.sum(x[:, c * t:(c + 1) * t], axis=1, keepdims=True)
        if tail is not None:
            s = s + tail
            total = total + tail
        outs.append(s)
        tail = total
    return jnp.concatenate(outs[::-1], axis=1), tail


def _sb_tile(q, kt, vt, tri, rsum, acc, valid):
    log_beta, log_keep = _sb_logs(_nt_dot(q, kt) * (HEAD_DIM ** -0.5))
    if valid is not None:
        log_keep = jnp.where(valid, log_keep, 0.0)
    after, total = _suffix_sums(log_keep, tri)
    a = jnp.exp(log_beta + (after + rsum))
    if valid is not None:
        a = jnp.where(valid, a, 0.0)
    return rsum + total, acc + _dot(a.astype(BF16), vt)


def _sb_prompt_kernel(seq, q_ref, k_ref, v_ref, tri_ref, o_ref):
    i = pl.program_id(1)
    tk = SB_KEY_TILE
    qb = SB_QBLK
    tri = tri_ref[...]
    q_row = i * qb + lax.broadcasted_iota(jnp.int32, (qb, tk), 0)
    lane = lax.broadcasted_iota(jnp.int32, (qb, tk), 1)
    heads = [slice(g * HEAD_DIM, (g + 1) * HEAD_DIM) for g in range(HEAD_GROUP)]
    qs = [q_ref[:, hs].astype(BF16) for hs in heads]

    def tile(start, carry, valid, width=tk, tri=tri):
        out = []
        for g, hs in enumerate(heads):
            kt = k_ref[pl.ds(start, width), hs].astype(BF16)
            vt = v_ref[pl.ds(start, width), hs].astype(BF16)
            out.append(_sb_tile(qs[g], kt, vt, tri, carry[g][0], carry[g][1], valid))
        return tuple(out)

    carry = tuple((jnp.zeros((qb, 1), F32), jnp.zeros((qb, HEAD_DIM), F32)) for _ in heads)

    c_diag = (i * qb) // tk
    d_start = pl.multiple_of(c_diag * tk, tk)
    carry = tile(d_start, carry, (d_start + lane) < q_row)
    carry = lax.fori_loop(0, c_diag, lambda it, cr: tile(pl.multiple_of((c_diag - 1 - it) * tk, tk), cr, None), carry)
    meta_valid = lax.broadcasted_iota(jnp.int32, (qb, META_TILE), 1) < N_META
    carry = tile(seq, carry, meta_valid, META_TILE, tri_ref[:META_TILE, :META_TILE])
    for g, hs in enumerate(heads):
        o_ref[:, hs] = carry[g][1]


def _sb_prompt(p, seq, n_rows):
    gw = HEAD_GROUP * HEAD_DIM
    tri = jnp.tril(jnp.ones((TRI, TRI), F32), -1).astype(BF16)
    return pl.pallas_call(
        functools.partial(_sb_prompt_kernel, seq),
        grid=(H_SB // HEAD_GROUP, seq // SB_QBLK),
        in_specs=[
            pl.BlockSpec((SB_QBLK, gw), lambda h, i: (i, P_Q_SB // gw + h)),
            pl.BlockSpec((n_rows, gw), lambda h, i: (0, P_K_SB // gw + h)),
            pl.BlockSpec((n_rows, gw), lambda h, i: (0, P_V_SB // gw + h)),
            pl.BlockSpec((TRI, TRI), lambda h, i: (0, 0)),
        ],
        out_specs=pl.BlockSpec((SB_QBLK, gw), lambda h, i: (i, h)),
        out_shape=jax.ShapeDtypeStruct((seq, W_SB), F32),
        compiler_params=_cparams(("parallel", "arbitrary")),
        name="sb_prompt",
    )(p, p, p, tri)


def _sortable_key(x):
    b = lax.bitcast_convert_type(x, jnp.int32)
    return jnp.where(b < 0, b ^ jnp.int32(0x7FFFFFFF), b)


def _idx_scores(qidx, wcols, ksmall):
    m = qidx.shape[0]
    n_pairs = H_IDX // 2
    lane = lax.broadcasted_iota(jnp.int32, ksmall.shape, 1)
    k_even = jnp.where(lane < D_IDX, ksmall, 0.0)
    k_odd = pltpu.roll(k_even, D_IDX, 1)
    q_stack = jnp.concatenate([qidx[:, hp * LANES:(hp + 1) * LANES] for hp in range(n_pairs)], axis=0)
    s_par = [_nt_dot(q_stack, kk.astype(BF16)) for kk in (k_even, k_odd)]
    acc = None
    for hp in range(n_pairs):
        for par in range(2):
            term = wcols[2 * hp + par] * jnp.maximum(s_par[par][hp * m:(hp + 1) * m], 0.0)
            acc = term if acc is None else acc + term
    return acc


def _topk_threshold(count_ge, shape, topk):
    def body(b, p):
        cand = p + lax.shift_left(jnp.int32(1), jnp.int32(31) - b)
        return jnp.where(count_ge(cand) >= topk, cand, p)

    return lax.fori_loop(0, 32, body, jnp.full(shape, INT_MIN, jnp.int32))


def _idx_prompt_kernel(seq, topk, qidx_ref, qsmall_ref, ksmall_ref, bias_ref, keys_s):
    i = pl.program_id(0)
    tk = KEY_TILE
    n_own = (i * QBLK + QBLK - 1) // tk + 1
    meta_start = seq
    qidx = qidx_ref[...].astype(BF16)
    qsmall = qsmall_ref[...] * (D_IDX ** -0.5)
    wcols = [qsmall[:, W_IDX_LANE + h:W_IDX_LANE + h + 1] for h in range(H_IDX)]
    q_row = i * QBLK + lax.broadcasted_iota(jnp.int32, (QBLK, tk), 0)
    lane = lax.broadcasted_iota(jnp.int32, (QBLK, tk), 1)
    meta_valid = lax.broadcasted_iota(jnp.int32, (QBLK, META_TILE), 1) < N_META

    def fill(start, valid, width):
        sc = _idx_scores(qidx, wcols, ksmall_ref[pl.ds(start, width), :])
        keys_s[:, pl.ds(start, width)] = jnp.where(valid, _sortable_key(sc), INT_MIN)

    def fill_body(c, _):
        start = pl.multiple_of(c * tk, tk)
        fill(start, (start + lane) <= q_row, tk)
        return 0

    lax.fori_loop(0, n_own, fill_body, 0)
    fill(meta_start, meta_valid, META_TILE)

    def count_ge(cand):
        counts = []
        for r0 in range(0, QBLK, COUNT_ROWS):
            cb = jnp.broadcast_to(cand[r0:r0 + COUNT_ROWS], (COUNT_ROWS, LANES))

            def add_tile(start, cnt, width, r0=r0, cb=cb):
                kt = keys_s[pl.ds(r0, COUNT_ROWS), pl.ds(start, width)]
                for t in range(width // LANES):
                    cnt = cnt + jnp.where(kt[:, t * LANES:(t + 1) * LANES] >= cb, 1.0, 0.0)
                return cnt

            cnt = lax.fori_loop(0, n_own, lambda c, cnt: add_tile(pl.multiple_of(c * tk, tk), cnt, tk),
                                jnp.zeros((COUNT_ROWS, LANES), F32))
            cnt = add_tile(meta_start, cnt, META_TILE)
            counts.append(jnp.sum(cnt, axis=1, keepdims=True))
        return jnp.concatenate(counts, axis=0)

    thr = _topk_threshold(count_ge, (QBLK, 1), float(topk))

    bias_ref[...] = jnp.full(bias_ref.shape, -jnp.inf, BF16)

    def emit(start, width):
        kt = keys_s[:, pl.ds(start, width)]
        sel = (kt >= thr) & (kt != INT_MIN)
        bias_ref[:, pl.ds(start, width)] = jnp.where(sel, 0.0, -jnp.inf).astype(BF16)

    def emit_body(c, _):
        emit(pl.multiple_of(c * tk, tk), tk)
        return 0

    lax.fori_loop(0, n_own, emit_body, 0)
    emit(meta_start, META_TILE)


def _idx_prompt(p, seq, n_rows):
    nq = seq // QBLK
    topk = min(INDEX_TOPK, (seq + N_META) // 4)
    return pl.pallas_call(
        functools.partial(_idx_prompt_kernel, seq, topk),
        grid=(nq,),
        in_specs=[
            pl.BlockSpec((QBLK, H_IDX * D_IDX), lambda i: (i, P_Q_IDX // (H_IDX * D_IDX))),
            pl.BlockSpec((QBLK, LANES), lambda i: (i, P_SMALL // LANES)),
            pl.BlockSpec((n_rows, LANES), lambda i: (0, P_SMALL // LANES)),
        ],
        out_specs=pl.BlockSpec((QBLK, n_rows), lambda i: (i, 0)),
        out_shape=jax.ShapeDtypeStruct((seq, n_rows), BF16),
        scratch_shapes=[pltpu.VMEM((QBLK, n_rows), jnp.int32)],
        compiler_params=_cparams(("parallel",)),
        name="idx_prompt",
    )(p, p, p)


def _sp_prompt_kernel(seq, q_ref, k_ref, v_ref, bias_ref, o_ref):
    i = pl.program_id(1)
    tk = SP_KEY_TILE
    n_own = (i * QBLK + QBLK - 1) // tk + 1
    heads = [slice(g * HEAD_DIM, (g + 1) * HEAD_DIM) for g in range(HEAD_GROUP)]
    qs = [q_ref[:, hs].astype(BF16) for hs in heads]

    def tile(start, carry, width=tk):
        bias = bias_ref[:, pl.ds(start, width)].astype(F32)
        out = []
        for g, hs in enumerate(heads):
            m, l, acc = carry[g]
            kt = k_ref[pl.ds(start, width), hs].astype(BF16)
            vt = v_ref[pl.ds(start, width), hs].astype(BF16)
            s = _nt_dot(qs[g], kt) * (HEAD_DIM ** -0.5) + bias
            m_new = jnp.maximum(m, jnp.max(s, axis=1, keepdims=True))
            alpha = jnp.exp(m - m_new)
            e = jnp.exp(s - m_new)
            l = alpha * l + jnp.sum(e, axis=1, keepdims=True)
            acc = alpha * acc + _dot(e.astype(BF16), vt)
            out.append((m_new, l, acc))
        return tuple(out)

    carry = tuple((jnp.full((QBLK, 1), -1e30, F32), jnp.zeros((QBLK, 1), F32), jnp.zeros((QBLK, HEAD_DIM), F32))
                  for _ in heads)
    carry = tile(seq, carry, META_TILE)
    carry = lax.fori_loop(0, n_own, lambda c, cr: tile(pl.multiple_of(c * tk, tk), cr), carry)
    for g, hs in enumerate(heads):
        o_ref[:, hs] = carry[g][2] / carry[g][1]


def _sp_prompt(p, bias, seq, n_rows):
    assert seq % SP_KEY_TILE == 0
    gw = HEAD_GROUP * HEAD_DIM
    return pl.pallas_call(
        functools.partial(_sp_prompt_kernel, seq),
        grid=(H_SP // HEAD_GROUP, seq // QBLK),
        in_specs=[
            pl.BlockSpec((QBLK, gw), lambda h, i: (i, P_Q_SP // gw + h)),
            pl.BlockSpec((n_rows, gw), lambda h, i: (0, P_K_SP // gw + h)),
            pl.BlockSpec((n_rows, gw), lambda h, i: (0, P_V_SP // gw + h)),
            pl.BlockSpec((QBLK, n_rows), lambda h, i: (i, 0)),
        ],
        out_specs=pl.BlockSpec((QBLK, gw), lambda h, i: (i, h)),
        out_shape=jax.ShapeDtypeStruct((seq, W_SP), F32),
        compiler_params=_cparams(("parallel", "arbitrary")),
        name="sp_prompt",
    )(p, p, p, bias)


Q_ROWS = 16


def _block_diag(qrow, n_heads):
    w = n_heads * HEAD_DIM
    r = lax.broadcasted_iota(jnp.int32, (Q_ROWS, w), 0)
    c = lax.broadcasted_iota(jnp.int32, (Q_ROWS, w), 1)
    return jnp.where((c // HEAD_DIM) == r, jnp.broadcast_to(qrow, (Q_ROWS, w)), 0.0)


def _diag_rows(x, n_heads):
    w = n_heads * HEAD_DIM
    r = lax.broadcasted_iota(jnp.int32, (Q_ROWS, w), 0)
    c = lax.broadcasted_iota(jnp.int32, (Q_ROWS, w), 1)
    return jnp.sum(jnp.where((c // HEAD_DIM) == r, x, 0.0), axis=0, keepdims=True)


def _page_rows(ref, n_heads, lead=(0, 0)):
    r = ref.at[lead]
    t = r.shape[0]
    flat = r.reshape(t * n_heads, HEAD_DIM)
    return jnp.concatenate([flat[pl.ds(h, t, stride=n_heads), :] for h in range(n_heads)], axis=1)


def _sb_sample_kernel(n_steps, pt_ref, q_ref, *rest):
    npg = SB_PAGES_PER_STEP
    k_refs = rest[:npg]
    v_refs = rest[npg:2 * npg]
    o_ref, rsum_s, acc_s = rest[2 * npg:]
    j = pl.program_id(1)

    @pl.when(j == 0)
    def _():
        rsum_s[...] = jnp.zeros_like(rsum_s)
        acc_s[...] = jnp.zeros_like(acc_s)

    q = _block_diag(q_ref[0], H_SB).astype(BF16)
    tri = _strict_lower_ones(PAGE_SIZE)
    z = jnp.concatenate([_nt_dot(q, _page_rows(k_refs[k], H_SB).astype(BF16)) for k in range(npg)], axis=0)
    log_beta, log_keep = _sb_logs(z * (HEAD_DIM ** -0.5))
    after, _ = _suffix_sums(log_keep, tri)
    totals = jnp.sum(log_keep, axis=1, keepdims=True)
    rsum = rsum_s[...]
    carried = []
    for k in range(npg):
        carried.append(rsum)
        rsum = rsum + totals[k * Q_ROWS:(k + 1) * Q_ROWS]
    rsum_s[...] = rsum
    a = jnp.exp(log_beta + (after + jnp.concatenate(carried, axis=0))).astype(BF16)
    acc = acc_s[...]
    for k in range(npg):
        acc = acc + _dot(a[k * Q_ROWS:(k + 1) * Q_ROWS], _page_rows(v_refs[k], H_SB).astype(BF16))
    acc_s[...] = acc

    @pl.when(j == n_steps - 1)
    def _():
        o_ref[0] = _diag_rows(acc, H_SB)


def _sb_sample(page_table, q, cache_k, cache_v):
    nb, n_pages = page_table.shape
    npg = SB_PAGES_PER_STEP
    n_steps = n_pages // npg

    def page_spec(k):
        return pl.BlockSpec((1, 1, PAGE_SIZE, H_SB, HEAD_DIM),
                            lambda b, j, pt: (pt[b, n_pages - 1 - (j * npg + k)], 0, 0, 0, 0))

    grid_spec = pltpu.PrefetchScalarGridSpec(
        num_scalar_prefetch=1,
        grid=(nb, n_steps),
        in_specs=[pl.BlockSpec((1, 1, W_SB), lambda b, j, pt: (b, 0, 0))]
        + [page_spec(k) for k in range(npg)] + [page_spec(k) for k in range(npg)],
        out_specs=pl.BlockSpec((1, 1, W_SB), lambda b, j, pt: (b, 0, 0)),
        scratch_shapes=[pltpu.VMEM((Q_ROWS, 1), F32), pltpu.VMEM((Q_ROWS, W_SB), F32)],
    )
    return pl.pallas_call(
        functools.partial(_sb_sample_kernel, n_steps),
        grid_spec=grid_spec,
        out_shape=jax.ShapeDtypeStruct((nb, 1, W_SB), F32),
        compiler_params=_cparams(("parallel", "arbitrary")),
        name="sb_sample",
    )(page_table, q, *([cache_k] * npg), *([cache_v] * npg))


def _idx_scores_sample_kernel(pt_ref, q_ref, w_ref, knew_ref, *rest):
    del pt_ref
    npg = IDX_PAGES_PER_STEP
    k_refs = rest[:npg]
    sc_ref, scnew_ref = rest[npg:]
    j = pl.program_id(1)
    q = q_ref[0].astype(BF16)
    w = w_ref[0]
    for k in range(npg):
        s = _dot(q, k_refs[k][0].astype(BF16)) * (D_IDX ** -0.5)
        sc_ref[0, pl.ds(j * npg + k, 1), :] = jnp.sum(w * jnp.maximum(s, 0.0), axis=0, keepdims=True)

    @pl.when(j == 0)
    def _():
        s_new = jnp.sum(q.astype(F32) * knew_ref[0].astype(BF16).astype(F32), axis=1, keepdims=True)
        sc_new = jnp.sum(w * jnp.maximum(s_new * (D_IDX ** -0.5), 0.0), axis=0, keepdims=True)
        scnew_ref[0] = jnp.broadcast_to(sc_new, (8, LANES))


def _idx_select_sample_kernel(nb, n_pages, topk, sc_ref, scnew_ref, idx_ref, info_ref, gt_s, eq_s, st_s):
    keys = _sortable_key(sc_ref[...])
    key_new = _sortable_key(scnew_ref[...])[:, :1, :1]

    def total(x):
        return jnp.sum(jnp.sum(x, axis=1, keepdims=True), axis=2, keepdims=True)

    def count_ge(cand):
        return total(jnp.where(keys >= cand, 1.0, 0.0)) + jnp.where(key_new >= cand, 1.0, 0.0)

    thr = _topk_threshold(count_ge, (nb, 1, 1), float(topk))
    gt = jnp.where(keys > thr, 1.0, 0.0)
    eq = jnp.where(keys == thr, 1.0, 0.0)
    gt_s[...] = gt
    eq_s[...] = eq
    n_gt = total(gt) + jnp.where(key_new > thr, 1.0, 0.0)
    need_all = float(topk) - n_gt
    n_eq_all = total(eq)
    new_gt = jnp.where(key_new > thr, 1.0, 0.0)
    new_eq = jnp.where(key_new == thr, 1.0, 0.0)
    lane3 = lax.broadcasted_iota(jnp.int32, (nb, 8, LANES), 2)
    st_s[...] = jnp.where(lane3 == 0, need_all,
                          jnp.where(lane3 == 1, n_eq_all, jnp.where(lane3 == 2, new_gt, new_eq)))

    r = lax.broadcasted_iota(jnp.int32, (n_pages, n_pages), 0)
    c = lax.broadcasted_iota(jnp.int32, (n_pages, n_pages), 1)
    pages_before = jnp.where(c < r, 1.0, 0.0).astype(BF16)
    pages_before_t = jnp.where(r < c, 1.0, 0.0).astype(BF16)
    rl = lax.broadcasted_iota(jnp.int32, (PAGE_SIZE, PAGE_SIZE), 0)
    cl = lax.broadcasted_iota(jnp.int32, (PAGE_SIZE, PAGE_SIZE), 1)
    lanes_before = jnp.where(rl < cl, 1.0, 0.0).astype(BF16)
    ones_q = jnp.ones((Q_ROWS, PAGE_SIZE), BF16)
    slot = lax.broadcasted_iota(jnp.int32, (topk, n_pages), 0).astype(F32)
    page_id = lax.broadcasted_iota(jnp.int32, (topk, n_pages), 1).astype(F32)
    tok_id = lax.broadcasted_iota(jnp.int32, (topk, PAGE_SIZE), 1).astype(F32)
    row8 = lax.broadcasted_iota(jnp.int32, (8, LANES), 0)

    def per_sequence(b, _):
        gt_b = gt_s[b]
        eq_b = eq_s[b]
        st = st_s[b]
        need = st[:1, 0:1]
        n_eq = st[:1, 1:2]

        def lane_rank(x):
            return _dot(x.astype(BF16), lanes_before)

        def page_offsets(x):
            cnt = jnp.broadcast_to(jnp.sum(x, axis=1, keepdims=True), x.shape)
            return _dot(pages_before, cnt.astype(BF16))

        eq_rank = page_offsets(eq_b) + lane_rank(eq_b)
        sel = jnp.maximum(gt_b, jnp.where(eq_rank < need, eq_b, 0.0))
        sel_bf = sel.astype(BF16)
        n_past = jnp.sum(jnp.sum(sel, axis=1, keepdims=True), axis=0, keepdims=True)
        sel_new = jnp.maximum(st[:1, 2:3], jnp.where(n_eq < need, st[:1, 3:4], 0.0))

        cnt_l = _nt_dot(ones_q, sel_bf)
        off_l = _dot(cnt_l.astype(BF16), pages_before_t)
        off = off_l[:1]
        end = off + cnt_l[:1]
        in_page = jnp.where((slot >= off) & (slot < end), 1.0, 0.0)
        page_of = jnp.sum(in_page * page_id, axis=1, keepdims=True)
        local = slot[:, :1] - jnp.sum(in_page * off, axis=1, keepdims=True)
        in_page_bf = in_page.astype(BF16)
        sel_rows = _dot(in_page_bf, sel_bf)
        rank_rows = _dot(in_page_bf, lane_rank(sel).astype(BF16))
        hit = jnp.where(rank_rows == local, sel_rows, 0.0)
        tok_of = jnp.sum(hit * tok_id, axis=1, keepdims=True)
        pos = page_of * PAGE_SIZE + tok_of
        idx_ref[b] = jnp.broadcast_to(pos, (topk, LANES)).astype(jnp.int32)
        info_ref[b] = jnp.where(row8 == 0, n_past, sel_new).astype(jnp.int32)
        return 0

    lax.fori_loop(0, nb, per_sequence, 0)


def _idx_sample(page_table, q, w, k_new, cache_k_idx):
    nb, n_pages = page_table.shape
    npg = IDX_PAGES_PER_STEP
    n_steps = n_pages // npg
    topk = min(INDEX_TOPK, (n_pages * PAGE_SIZE + 1) // 4)

    def page_spec(k):
        return pl.BlockSpec((1, D_IDX, PAGE_SIZE), lambda b, j, pt: (pt[b, j * npg + k], 0, 0))

    grid_spec = pltpu.PrefetchScalarGridSpec(
        num_scalar_prefetch=1,
        grid=(nb, n_steps),
        in_specs=[
            pl.BlockSpec((1, H_IDX, D_IDX), lambda b, j, pt: (b, 0, 0)),
            pl.BlockSpec((1, H_IDX, 1), lambda b, j, pt: (b, 0, 0)),
            pl.BlockSpec((1, 1, D_IDX), lambda b, j, pt: (b, 0, 0)),
        ] + [page_spec(k) for k in range(npg)],
        out_specs=[
            pl.BlockSpec((1, n_pages, PAGE_SIZE), lambda b, j, pt: (b, 0, 0)),
            pl.BlockSpec((1, 8, LANES), lambda b, j, pt: (b, 0, 0)),
        ],
    )
    scores, sc_new = pl.pallas_call(
        _idx_scores_sample_kernel,
        grid_spec=grid_spec,
        out_shape=[jax.ShapeDtypeStruct((nb, n_pages, PAGE_SIZE), F32),
                   jax.ShapeDtypeStruct((nb, 8, LANES), F32)],
        compiler_params=_cparams(("parallel", "arbitrary")),
        name="idx_sample_scores",
    )(page_table, q, w, k_new, *([cache_k_idx] * npg))
    return pl.pallas_call(
        functools.partial(_idx_select_sample_kernel, nb, n_pages, topk),
        out_shape=[jax.ShapeDtypeStruct((nb, topk, LANES), jnp.int32),
                   jax.ShapeDtypeStruct((nb, 8, LANES), jnp.int32)],
        scratch_shapes=[pltpu.VMEM((nb, n_pages, PAGE_SIZE), F32), pltpu.VMEM((nb, n_pages, PAGE_SIZE), F32),
                        pltpu.VMEM((nb, 8, LANES), F32)],
        compiler_params=pltpu.CompilerParams(vmem_limit_bytes=VMEM_LIMIT_BYTES),
        name="idx_sample_select",
    )(scores, sc_new)


def _sp_sample_kernel(topk, n_pages, pt_ref, idx_ref, npast_ref, selnew_ref, q_ref, knew_ref, vnew_ref,
                      k_hbm, v_hbm, o_ref, kbuf, vbuf, sem):
    b = pl.program_id(0)
    nb = pl.num_programs(0)

    def row_copy(src, dst, slot, seq_b, r):
        pos = idx_ref[seq_b * topk + r]
        page = pt_ref[seq_b * n_pages + lax.shift_right_logical(pos, PAGE_SIZE.bit_length() - 1)]
        tok = pos & (PAGE_SIZE - 1)
        return pltpu.make_async_copy(src.at[page, 0, pl.ds(tok, 1)], dst.at[slot, pl.ds(r, 1)], sem.at[slot])

    def start_all(seq_b, slot):
        def body(r, _):
            row_copy(k_hbm, kbuf, slot, seq_b, r).start()
            row_copy(v_hbm, vbuf, slot, seq_b, r).start()
            return 0
        lax.fori_loop(0, topk, body, 0)

    def wait_all(seq_b, slot):
        def body(r, _):
            row_copy(k_hbm, kbuf, slot, seq_b, r).wait()
            row_copy(v_hbm, vbuf, slot, seq_b, r).wait()
            return 0
        lax.fori_loop(0, topk, body, 0)

    slot = b % 2

    @pl.when(b == 0)
    def _():
        start_all(0, 0)

    @pl.when(b + 1 < nb)
    def _():
        start_all(b + 1, 1 - slot)

    wait_all(b, slot)

    qbd = _block_diag(q_ref[0], H_SP)
    scale = HEAD_DIM ** -0.5
    ksel = _page_rows(kbuf, H_SP, (slot,)).astype(BF16)
    vsel = _page_rows(vbuf, H_SP, (slot,)).astype(BF16)
    s = _nt_dot(qbd.astype(BF16), ksel) * scale
    lane = lax.broadcasted_iota(jnp.int32, s.shape, 1)
    s = jnp.where(lane < npast_ref[b], s, -jnp.inf)
    s_new = jnp.sum(qbd * knew_ref[0], axis=1, keepdims=True) * scale
    s_new = jnp.where(selnew_ref[b] > 0, s_new, -jnp.inf)
    m = jnp.maximum(jnp.max(s, axis=1, keepdims=True), s_new)
    e = jnp.exp(s - m)
    e_new = jnp.exp(s_new - m)
    denom = jnp.sum(e, axis=1, keepdims=True) + e_new
    acc = _dot(e.astype(BF16), vsel) + e_new * vnew_ref[0]
    o_ref[0] = _diag_rows(acc / denom, H_SP)


def _sp_sample(page_table, idx, n_past, sel_new, q, k_new, v_new, cache_k, cache_v, topk):
    nb = q.shape[0]
    n_pages = page_table.shape[0] // nb
    vec = pl.BlockSpec((1, 1, W_SP), lambda b, *_: (b, 0, 0))
    grid_spec = pltpu.PrefetchScalarGridSpec(
        num_scalar_prefetch=4,
        grid=(nb,),
        in_specs=[vec, vec, vec, pl.BlockSpec(memory_space=pl.ANY), pl.BlockSpec(memory_space=pl.ANY)],
        out_specs=vec,
        scratch_shapes=[pltpu.VMEM((2, topk, H_SP, HEAD_DIM), F32), pltpu.VMEM((2, topk, H_SP, HEAD_DIM), F32),
                        pltpu.SemaphoreType.DMA((2,))],
    )
    return pl.pallas_call(
        functools.partial(_sp_sample_kernel, topk, n_pages),
        grid_spec=grid_spec,
        out_shape=jax.ShapeDtypeStruct((nb, 1, W_SP), F32),
        compiler_params=_cparams(("arbitrary",)),
        name="sp_sample",
    )(page_table, idx, n_past, sel_new, q, k_new, v_new, cache_k, cache_v)


def _merge_kernel(osb_ref, osp_ref, gsb_ref, gsp_ref, h_ref, wsb_ref, wsp_ref, wout_ref, o_ref, y_s):
    j = pl.program_id(1)

    @pl.when(j == 0)
    def _():
        y_sb = _dot(osb_ref[...].astype(BF16), wsb_ref[...])
        y_sp = _dot(osp_ref[...].astype(BF16), wsp_ref[...])
        y_s[...] = (gsb_ref[...] * y_sb + gsp_ref[...] * y_sp).astype(BF16)

    o_ref[...] = h_ref[...] + _dot(y_s[...], wout_ref[...])


def _merge(o_sb, o_sp, g_sb, g_sp, h, w_sb, w_sp, w_out, n_rows, tm, g_cols):
    tn = 512
    return pl.pallas_call(
        _merge_kernel,
        grid=(n_rows // tm, D_MODEL // tn),
        in_specs=[
            pl.BlockSpec((tm, W_SB), lambda i, j: (i, 0)),
            pl.BlockSpec((tm, W_SP), lambda i, j: (i, 0)),
            pl.BlockSpec((tm, D_MODEL), lambda i, j: (i, g_cols[0])),
            pl.BlockSpec((tm, D_MODEL), lambda i, j: (i, g_cols[1])),
            pl.BlockSpec((tm, tn), lambda i, j: (i, j)),
            pl.BlockSpec((W_SB, D_MODEL), lambda i, j: (0, 0)),
            pl.BlockSpec((W_SP, D_MODEL), lambda i, j: (0, 0)),
            pl.BlockSpec((D_MODEL, tn), lambda i, j: (0, j)),
        ],
        out_specs=pl.BlockSpec((tm, tn), lambda i, j: (i, j)),
        out_shape=jax.ShapeDtypeStruct((n_rows, D_MODEL), F32),
        scratch_shapes=[pltpu.VMEM((tm, D_MODEL), BF16)],
        compiler_params=_cparams(("parallel", "arbitrary")),
        name="merge",
    )(o_sb, o_sp, g_sb, g_sp, h, w_sb, w_sp, w_out)


def kernel(x_prompt, x_sample, cache_k_sb, cache_v_sb, cache_k_sp, cache_v_sp, cache_k_idx, page_table, meta_tokens, g_ffn1, w_ffn1_gate, w_ffn1_up, w_ffn1_down, g_mix, w_in, w_proj_sb, w_proj_sp, w_out, g_ffn2, w_ffn2_gate, w_ffn2_up, w_ffn2_down, g_final):
    batch, seq, _ = x_prompt.shape
    nb, dec_seq, _ = x_sample.shape
    n_pool = cache_k_sb.shape[0]
    n_pages = page_table.shape[1]
    past_len = n_pages * PAGE_SIZE
    assert batch == 1 and dec_seq == 1 and g_ffn1.shape[0] == 1
    assert seq % ROW_TILE == 0 and N_META + nb <= ROW_TILE
    n_rows = seq + ROW_TILE
    samp0 = seq + N_META

    h0 = jnp.concatenate([x_prompt[0], meta_tokens.astype(F32), x_sample[:, 0, :],
                          jnp.zeros((n_rows - samp0 - nb, D_MODEL), F32)], axis=0)
    pos = jnp.concatenate([N_META + jnp.arange(seq, dtype=jnp.int32), jnp.arange(N_META, dtype=jnp.int32),
                           jnp.full((nb,), past_len, jnp.int32), jnp.zeros((n_rows - samp0 - nb,), jnp.int32)])
    tables = _rope_tables(pos)
    wi = w_in[0]
    o_k_idx = 7168
    o_g_sb = o_k_idx + D_IDX + H_IDX
    w_proj = jnp.concatenate([
        wi[:, o_g_sb:],
        wi[:, 0:3 * W_SB],
        wi[:, 3 * W_SB + 2 * W_SP:3 * W_SB + 3 * W_SP],
        wi[:, 3 * W_SB:3 * W_SB + 2 * W_SP],
        wi[:, 3 * W_SB + 3 * W_SP:o_k_idx],
        wi[:, o_k_idx:o_g_sb], jnp.zeros((D_MODEL, PROJ_TILE - D_IDX - H_IDX), F32),
    ], axis=1).astype(BF16)
    bf = lambda w: w[0].astype(BF16)
    row = lambda g: g.reshape(1, D_MODEL)

    h1 = _ffn(h0, n_rows, ROW_TILE, row(g_ffn1[0]), bf(w_ffn1_gate), bf(w_ffn1_up), bf(w_ffn1_down))
    p = _project(h1, row(g_mix[0]), w_proj, tables, n_rows)

    o_sb_p = _sb_prompt(p, seq, n_rows)
    bias = _idx_prompt(p, seq, n_rows)
    o_sp_p = _sp_prompt(p, bias, seq, n_rows)

    ps = p[samp0:samp0 + nb]
    seg = lambda base, w: ps[:, base:base + w]
    o_sb_s = _sb_sample(page_table, seg(P_Q_SB, W_SB).reshape(nb, 1, W_SB), cache_k_sb, cache_v_sb)
    idx, info = _idx_sample(page_table, seg(P_Q_IDX, H_IDX * D_IDX).reshape(nb, H_IDX, D_IDX),
                            seg(P_SMALL + W_IDX_LANE, H_IDX).reshape(nb, H_IDX, 1),
                            seg(P_SMALL, D_IDX).reshape(nb, 1, D_IDX),
                            jnp.swapaxes(cache_k_idx.reshape(n_pool, PAGE_SIZE, D_IDX), 1, 2))
    topk = idx.shape[1]
    o_sp_s = _sp_sample(page_table.reshape(-1), idx[:, :, 0].reshape(-1), info[:, 0, 0], info[:, 1, 0],
                        seg(P_Q_SP, W_SP).reshape(nb, 1, W_SP), seg(P_K_SP, W_SP).reshape(nb, 1, W_SP),
                        seg(P_V_SP, W_SP).reshape(nb, 1, W_SP), cache_k_sp, cache_v_sp, topk)

    wsb, wsp, wo = bf(w_proj_sb), bf(w_proj_sp), bf(w_out)
    ffn2 = (row(g_ffn2[0]), bf(w_ffn2_gate), bf(w_ffn2_up), bf(w_ffn2_down), row(g_final))
    h2_p = _merge(o_sb_p, o_sp_p, p, p, h1, wsb, wsp, wo, seq, ROW_TILE, (P_G_SB // D_MODEL, P_G_SP // D_MODEL))
    y_p = _ffn(h2_p, seq, ROW_TILE, *ffn2)
    h2_s = _merge(o_sb_s.reshape(nb, W_SB), o_sp_s.reshape(nb, W_SP), seg(P_G_SB, D_MODEL), seg(P_G_SP, D_MODEL),
                  h1[samp0:samp0 + nb], wsb, wsp, wo, nb, nb, (0, 0))
    y_s = _ffn(h2_s, nb, nb, *ffn2)

    def prompt_rows(base, w):
        return jnp.concatenate([p[seq:seq + N_META, base:base + w], p[:seq, base:base + w]], axis=0)

    def kv_p(base, nh):
        return prompt_rows(base, nh * HEAD_DIM).reshape(1, 1, seq + N_META, nh, HEAD_DIM)

    def kv_s(base, nh):
        return seg(base, nh * HEAD_DIM).reshape(nb, 1, 1, nh, HEAD_DIM)

    return (y_p.reshape(1, seq, D_MODEL), y_s.reshape(nb, 1, D_MODEL),
            kv_p(P_K_SB, H_SB), kv_p(P_V_SB, H_SB), kv_p(P_K_SP, H_SP), kv_p(P_V_SP, H_SP),
            prompt_rows(P_SMALL, D_IDX).reshape(1, 1, seq + N_META, D_IDX),
            kv_s(P_K_SB, H_SB), kv_s(P_V_SB, H_SB), kv_s(P_K_SP, H_SP), kv_s(P_V_SP, H_SP),
            seg(P_SMALL, D_IDX).reshape(nb, 1, 1, D_IDX))
```
